```python
import math
import jax, jax.numpy as jnp
from jax import lax
import numpy as np

D_MODEL = 1024
BATCH = 2
SEQ = 8192
DEPTH = 1
DEC_BATCH = 32
DEC_SEQ = 1
PAST_LEN = 8192
PAGE_SIZE = 128

HEAD_DIM = 64
W_MIX = D_MODEL
W_SB = W_MIX // 2
W_FOX = W_MIX - W_SB
H_SB = W_SB // HEAD_DIM
H_FOX = W_FOX // HEAD_DIM
N_HEADS = H_SB + H_FOX
D_FF = 4 * D_MODEL
N_META = 16
Q_BLOCK = 128
N_IN = 3 * W_SB + 3 * W_FOX + H_FOX
EPS = 1e-6
NEG = -1e30
FORGET_BIAS_INIT = 2.0

kernel_name = "hymba_stickbreaking_fox_decode_step"


def rms_norm(x, g):
    xf = x.astype(jnp.float32)
    y = xf * lax.rsqrt(jnp.mean(xf * xf, axis=-1, keepdims=True) + EPS)
    return (y * g.astype(jnp.float32)).astype(x.dtype)


def project(h, w_in, b_f, q_g, k_g):
    lead = h.shape[:-1]
    p = h @ w_in
    cuts = [int(c) for c in np.cumsum([W_SB, W_SB, W_SB, W_FOX, W_FOX, W_FOX])]
    q_sb, k_sb, v_sb, q_fx, k_fx, v_fx, f = jnp.split(p, cuts, axis=-1)
    heads = lambda t, n: t.reshape(*lead, n, HEAD_DIM)
    q_sb, k_sb, v_sb = heads(q_sb, H_SB), heads(k_sb, H_SB), heads(v_sb, H_SB)
    q_fx = rms_norm(heads(q_fx, H_FOX), q_g)
    k_fx = rms_norm(heads(k_fx, H_FOX), k_g)
    v_fx = heads(v_fx, H_FOX)
    logf = jax.nn.log_sigmoid((f + b_f).astype(jnp.float32))
    return q_sb, k_sb, v_sb, q_fx, k_fx, v_fx, logf


def stick_breaking(q, k, v, qpos, kpos, kvalid):
    z = jnp.einsum('bqhd,bkhd->bhqk', q, k).astype(jnp.float32) / math.sqrt(HEAD_DIM)
    mask = (kpos[None, :] < qpos[:, None]) & kvalid[None, :]
    log_keep = jnp.where(mask, jax.nn.log_sigmoid(-z), 0.0)
    after = lax.cumsum(log_keep, axis=3, reverse=True) - log_keep
    a = jnp.where(mask, jnp.exp(jax.nn.log_sigmoid(z) + after), 0.0)
    return jnp.einsum('bhqk,bkhd->bqhd', a.astype(v.dtype), v)


def forgetting_attn(q, k, v, cq, ck, qpos, kpos, kvalid):
    s = jnp.einsum('bqhd,bkhd->bhqk', q, k).astype(jnp.float32) / math.sqrt(HEAD_DIM)
    s = s + (jnp.swapaxes(cq, 1, 2)[:, :, :, None] - jnp.swapaxes(ck, 1, 2)[:, :, None, :])
    mask = (kpos[None, :] <= qpos[:, None]) & kvalid[None, :]
    p = jax.nn.softmax(jnp.where(mask, s, NEG), axis=-1)
    return jnp.einsum('bhqk,bkhd->bqhd', p.astype(v.dtype), v)


def prompt_mixers(q_sb, k_sb, v_sb, q_fx, k_fx, v_fx, logf):
    pad = (-N_META) % Q_BLOCK
    padf = lambda t: jnp.pad(t, [(0, 0), (pad, 0)] + [(0, 0)] * (t.ndim - 2))
    q_sb, k_sb, v_sb, q_fx, k_fx, v_fx, logf = [padf(t) for t in (q_sb, k_sb, v_sb, q_fx, k_fx, v_fx, logf)]
    b, t_len = q_sb.shape[0], q_sb.shape[1]
    nb = t_len // Q_BLOCK
    kpos = jnp.arange(t_len)
    kvalid = kpos >= pad
    c = jnp.cumsum(logf, axis=1)
    to_blocks = lambda t: jnp.moveaxis(t.reshape(b, nb, Q_BLOCK, *t.shape[2:]), 1, 0)

    def block(args):
        i, qs, qf, cqb = args
        qpos = i * Q_BLOCK + jnp.arange(Q_BLOCK)
        o_sb = stick_breaking(qs, k_sb, v_sb, qpos, kpos, kvalid)
        o_fx = forgetting_attn(qf, k_fx, v_fx, cqb, c, qpos, kpos, kvalid)
        return o_sb, o_fx

    o_sb, o_fx = lax.map(block, (jnp.arange(nb), to_blocks(q_sb), to_blocks(q_fx), to_blocks(c)))
    from_blocks = lambda o: jnp.moveaxis(o, 0, 1).reshape(b, t_len, *o.shape[3:])[:, pad:]
    return from_blocks(o_sb), from_blocks(o_fx)


def gather_pages(pool, page_table):
    g = pool[page_table]
    return g.reshape(g.shape[0], g.shape[1] * g.shape[2], *g.shape[3:])


def merge(o_sb, o_fx, g_sb, g_fx, w_out):
    lead = o_sb.shape[:2]
    o = jnp.concatenate([rms_norm(o_sb.reshape(*lead, W_SB), g_sb),
                         rms_norm(o_fx.reshape(*lead, W_FOX), g_fx)], axis=-1)
    return o @ w_out


def mlp(x, g, w_up, w_down):
    h = rms_norm(x, g)
    return jnp.square(jax.nn.relu(h @ w_up)) @ w_down


def setup_inputs(seed: int = 0) -> dict:
    key = jax.random.key(seed)
    ks = jax.random.split(key, 24)
    n_pages = PAST_LEN // PAGE_SIZE
    n_used = DEC_BATCH * n_pages
    n_pool = n_used + (n_used + 3) // 4
    nrm = lambda k, shape, s=1.0: jax.random.normal(k, shape, jnp.float32) * s
    gain = lambda k, shape: 1.0 + nrm(k, shape, 0.05)
    page_table = jax.random.permutation(ks[5], n_pool)[:n_used].reshape(DEC_BATCH, n_pages).astype(jnp.int32)
    return {
        "x_prompt": nrm(ks[0], (BATCH, SEQ, D_MODEL)),
        "x_sample": nrm(ks[1], (DEC_BATCH, DEC_SEQ, D_MODEL)),
        "cache_k": nrm(ks[2], (DEPTH, n_pool, PAGE_SIZE, N_HEADS, HEAD_DIM)),
        "cache_v": nrm(ks[3], (DEPTH, n_pool, PAGE_SIZE, N_HEADS, HEAD_DIM)),
        "cache_logf": jax.nn.log_sigmoid(FORGET_BIAS_INIT + nrm(ks[4], (DEPTH, n_pool, PAGE_SIZE, H_FOX))),
        "page_table": page_table,
        "meta_tokens": nrm(ks[6], (N_META, D_MODEL)),
        "norm_attn": gain(ks[7], (DEPTH, D_MODEL)),
        "w_in": nrm(ks[8], (DEPTH, D_MODEL, N_IN), D_MODEL ** -0.5),
        "b_forget": FORGET_BIAS_INIT + nrm(ks[9], (DEPTH, H_FOX), 0.5),
        "q_norm": gain(ks[10], (DEPTH, HEAD_DIM)),
        "k_norm": gain(ks[11], (DEPTH, HEAD_DIM)),
        "out_norm_sb": gain(ks[12], (DEPTH, W_SB)),
        "out_norm_fox": gain(ks[13], (DEPTH, W_FOX)),
        "w_out": nrm(ks[14], (DEPTH, W_MIX, D_MODEL), W_MIX ** -0.5),
        "norm_mlp": gain(ks[15], (DEPTH, D_MODEL)),
        "w_up": nrm(ks[16], (DEPTH, D_MODEL, D_FF), D_MODEL ** -0.5),
        "w_down": nrm(ks[17], (DEPTH, D_FF, D_MODEL), D_FF ** -0.5),
    }


def reference(x_prompt, x_sample, cache_k, cache_v, cache_logf, page_table, meta_tokens,
              norm_attn, w_in, b_forget, q_norm, k_norm, out_norm_sb, out_norm_fox,
              w_out, norm_mlp, w_up, w_down):
    b = x_prompt.shape[0]
    xp = jnp.concatenate([jnp.broadcast_to(meta_tokens[None].astype(x_prompt.dtype), (b, N_META, D_MODEL)),
                          x_prompt], axis=1)
    xs = x_sample
    nkp, nvp, nlp, nks, nvs, nls = [], [], [], [], [], []
    for l in range(DEPTH):
        hp = rms_norm(xp, norm_attn[l])
        q_sb, k_sb, v_sb, q_fx, k_fx, v_fx, lf = project(hp, w_in[l], b_forget[l], q_norm[l], k_norm[l])
        o_sb, o_fx = prompt_mixers(q_sb, k_sb, v_sb, q_fx, k_fx, v_fx, lf)
        xp = xp + merge(o_sb, o_fx, out_norm_sb[l], out_norm_fox[l], w_out[l])
        xp = xp + mlp(xp, norm_mlp[l], w_up[l], w_down[l])
        nkp.append(jnp.concatenate([k_sb, k_fx], axis=2))
        nvp.append(jnp.concatenate([v_sb, v_fx], axis=2))
        nlp.append(lf)

        hs = rms_norm(xs, norm_attn[l])
        q_sb, k_sb, v_sb, q_fx, k_fx, v_fx, lf = project(hs, w_in[l], b_forget[l], q_norm[l], k_norm[l])
        k_new = jnp.concatenate([k_sb, k_fx], axis=2)
        v_new = jnp.concatenate([v_sb, v_fx], axis=2)
        k_all = jnp.concatenate([gather_pages(cache_k[l], page_table).astype(k_new.dtype), k_new], axis=1)
        v_all = jnp.concatenate([gather_pages(cache_v[l], page_table).astype(v_new.dtype), v_new], axis=1)
        past = k_all.shape[1] - k_new.shape[1]
        lf_all = jnp.concatenate([gather_pages(cache_logf[l], page_table).astype(jnp.float32), lf], axis=1)
        c = jnp.cumsum(lf_all, axis=1)
        kpos = jnp.arange(k_all.shape[1])
        qpos = past + jnp.arange(k_new.shape[1])
        kvalid = jnp.ones((k_all.shape[1],), dtype=bool)
        o_sb = stick_breaking(q_sb, k_all[:, :, :H_SB], v_all[:, :, :H_SB], qpos, kpos, kvalid)
        o_fx = forgetting_attn(q_fx, k_all[:, :, H_SB:], v_all[:, :, H_SB:], c[:, past:], c,
                               qpos, kpos, kvalid)
        xs = xs + merge(o_sb, o_fx, out_norm_sb[l], out_norm_fox[l], w_out[l])
        xs = xs + mlp(xs, norm_mlp[l], w_up[l], w_down[l])
        nks.append(k_new)
        nvs.append(v_new)
        nls.append(lf)
    y_prompt = xp[:, N_META:]
    y_sample = xs
    return (y_prompt, y_sample, jnp.stack(nkp), jnp.stack(nvp), jnp.stack(nlp),
            jnp.stack(nks), jnp.stack(nvs), jnp.stack(nls))
```

```python
import functools
import math

import jax
import jax.numpy as jnp
from jax import lax
from jax.experimental import pallas as pl
from jax.experimental.pallas import tpu as pltpu

F32 = jnp.float32
BF16 = jnp.bfloat16

HEAD_DIM = 64
N_META = 16
EPS = 1e-6
NEG = -1e30
LANES = 128
BLK = 128
VMEM_LIMIT = 56 * 1024 * 1024


def _params(sem):
    return pltpu.CompilerParams(dimension_semantics=sem, vmem_limit_bytes=VMEM_LIMIT)


def _dot(a, b):
    return jnp.dot(a, b, preferred_element_type=F32)


def _dot_nt(a, b):
    return lax.dot_general(a, b, (((1,), (1,)), ((), ())), preferred_element_type=F32)


def _dot_exact(a, b):
    return jnp.dot(a, b, preferred_element_type=F32, precision=lax.Precision.HIGHEST)


def _softplus(z):
    return jnp.maximum(z, 0.0) + jnp.log(1.0 + jnp.exp(-jnp.abs(z)))


def _const_spec(shape):
    nd = len(shape)
    return pl.BlockSpec(shape, lambda *_: (0,) * nd, pipeline_mode=pl.Buffered(1))


def _suffix_ones(n, dtype):
    r = lax.broadcasted_iota(jnp.int32, (n, 2 * n), 0)
    c = lax.broadcasted_iota(jnp.int32, (n, 2 * n), 1)
    return jnp.where((r > c) | (c >= n), 1.0, 0.0).astype(dtype)


def _proj_kernel(x_ref, ga_ref, wq_ref, wkvf_ref, bf_ref, gq_ref, gk_ref, grp_ref,
                 qb_ref, ktb_ref, vtb_ref, kt32_ref, vt32_ref, lft_ref):
    x = x_ref[...]
    d = x.shape[1]
    w_fox = d // 2
    n_fox = w_fox // HEAD_DIM
    tm = x.shape[0]
    inv = lax.rsqrt(jnp.mean(x * x, axis=-1, keepdims=True) + EPS)
    h = (x * inv * ga_ref[...]).astype(BF16)
    scale = 1.0 / math.sqrt(HEAD_DIM)

    q = _dot(h, wq_ref[...])
    q_fx = q[:, w_fox:]
    sq = q_fx * q_fx
    hi = sq.astype(BF16)
    lo = (sq - hi.astype(F32)).astype(BF16)
    grp = grp_ref[...]
    ms = (_dot(hi, grp) + _dot(lo, grp)) * (1.0 / HEAD_DIM)
    q_fx = q_fx * lax.rsqrt(ms + EPS) * gq_ref[...] * scale
    qb_ref[...] = jnp.concatenate([q[:, :w_fox] * scale, q_fx], axis=1).astype(BF16)

    kvf = _dot_nt(wkvf_ref[...], h)
    k_fx = kvf[w_fox:d].reshape(n_fox, HEAD_DIM, tm)
    k_fx = k_fx * lax.rsqrt(jnp.mean(k_fx * k_fx, axis=1, keepdims=True) + EPS)
    k_fx = k_fx.reshape(w_fox, tm) * gk_ref[...]
    kt = jnp.concatenate([kvf[:w_fox], k_fx], axis=0)
    vt = kvf[d:2 * d]
    ktb_ref[0] = kt.astype(BF16)
    vtb_ref[0] = vt.astype(BF16)
    kt32_ref[0] = kt
    vt32_ref[0] = vt

    y = kvf[2 * d:2 * d + n_fox] + bf_ref[...]
    lft_ref[0] = jnp.minimum(y, 0.0) - jnp.log(1.0 + jnp.exp(-jnp.abs(y)))


def _proj_call(x2d, n_batch, tm, ga, wq, wkvf, bf, gq, gk, grp):
    rows, d = x2d.shape
    seq = rows // n_batch
    per = seq // tm
    n_f = bf.shape[0]
    row_spec = pl.BlockSpec((tm, d), lambda i: (i, 0))
    t_spec = lambda r: pl.BlockSpec((1, r, tm), lambda i: (i // per, 0, i % per))
    t_shape = lambda r, dt: jax.ShapeDtypeStruct((n_batch, r, seq), dt)
    return pl.pallas_call(
        _proj_kernel,
        grid=(rows // tm,),
        in_specs=[row_spec] + [_const_spec(a.shape) for a in (ga, wq, wkvf, bf, gq, gk, grp)],
        out_specs=[row_spec, t_spec(d), t_spec(d), t_spec(d), t_spec(d), t_spec(n_f)],
        out_shape=[jax.ShapeDtypeStruct((rows, d), BF16), t_shape(d, BF16), t_shape(d, BF16),
                   t_shape(d, F32), t_shape(d, F32), t_shape(n_f, F32)],
        compiler_params=_params(("arbitrary",)),
        name="proj",
    )(x2d, ga, wq, wkvf, bf, gq, gk, grp)


def _cumsum_kernel(lf_ref, lfmeta_ref, c_ref, cmeta_ref):
    n_batch, n_h, seq = lf_ref.shape
    r = lax.broadcasted_iota(jnp.int32, (LANES, LANES), 0)
    c = lax.broadcasted_iota(jnp.int32, (LANES, LANES), 1)
    tri = (r <= c).astype(F32)
    cmeta = _dot_exact(lfmeta_ref[...], tri)
    cmeta_ref[...] = cmeta
    base = jnp.broadcast_to(cmeta[:, LANES - 1:LANES], (n_h, LANES))
    for b in range(n_batch):
        def body(i, carry):
            off = pl.multiple_of(i * LANES, LANES)
            pre = _dot_exact(lf_ref[b, :, pl.ds(off, LANES)], tri) + carry
            c_ref[b, :, pl.ds(off, LANES)] = pre
            return jnp.broadcast_to(pre[:, LANES - 1:LANES], (n_h, LANES))
        lax.fori_loop(0, seq // LANES, body, base)


def _cumsum_call(lft_main, lft_meta):
    return pl.pallas_call(
        _cumsum_kernel,
        out_shape=[jax.ShapeDtypeStruct(lft_main.shape, F32), jax.ShapeDtypeStruct(lft_meta.shape, F32)],
        compiler_params=pltpu.CompilerParams(vmem_limit_bytes=VMEM_LIMIT),
        name="cumsum",
    )(lft_main, lft_meta)


def _attn_specs(pair_offset, seq):
    q_spec = pl.BlockSpec((1, BLK, LANES), lambda b, p, i: (b, i, p + pair_offset))
    kv_spec = pl.BlockSpec((1, LANES, seq), lambda b, p, i: (b, p + pair_offset, 0))
    meta_spec = pl.BlockSpec((LANES, BLK), lambda b, p, i: (p + pair_offset, 0))
    return q_spec, kv_spec, meta_spec


def _split_heads(q_ref):
    q = q_ref[0].astype(F32)
    low = lax.broadcasted_iota(jnp.int32, q.shape, 1) < HEAD_DIM
    return jnp.where(low, q, 0.0).astype(BF16), jnp.where(low, 0.0, q).astype(BF16)


def _sb_kernel(q_ref, kt_ref, vt_ref, ktm_ref, vtm_ref, o_ref):
    i = pl.program_id(2)
    row = lax.broadcasted_iota(jnp.int32, (BLK, BLK), 0)
    col = lax.broadcasted_iota(jnp.int32, (BLK, BLK), 1)
    strict = col < row
    meta_valid = col < N_META
    suffix_ones = _suffix_ones(BLK, BF16)
    q_heads = _split_heads(q_ref)

    def tile(qh, ktb, vtb, carry, acc, mask):
        z = _dot(qh, ktb)
        sp = _softplus(z)
        if mask is not None:
            sp = jnp.where(mask, sp, 0.0)
        r = _dot(sp.astype(BF16), suffix_ones)
        arg = (z - sp) - r[:, :BLK] - carry
        if mask is not None:
            arg = jnp.where(mask, arg, NEG)
        acc = acc + _dot_nt(jnp.exp(arg).astype(BF16), vtb)
        return carry + r[:, BLK:], acc

    def step(ktb, vtb, state, mask):
        c0, a0, c1, a1 = state
        c0, a0 = tile(q_heads[0], ktb, vtb, c0, a0, mask)
        c1, a1 = tile(q_heads[1], ktb, vtb, c1, a1, mask)
        return c0, a0, c1, a1

    def main_step(off, state, mask):
        return step(kt_ref[0, :, pl.ds(off, BLK)], vt_ref[0, :, pl.ds(off, BLK)], state, mask)

    zeros = jnp.zeros((BLK, LANES), F32)
    state = main_step(pl.multiple_of(i * BLK, BLK), (zeros, zeros, zeros, zeros), strict)
    state = lax.fori_loop(0, i, lambda j, st: main_step(pl.multiple_of((i - 1 - j) * BLK, BLK), st, None),
                          state)
    state = step(ktm_ref[...], vtm_ref[...], state, meta_valid)
    low = lax.broadcasted_iota(jnp.int32, (BLK, LANES), 1) < HEAD_DIM
    o_ref[0] = jnp.where(low, state[1], state[3])


def _sb_call(qb, ktb, vtb, ktm, vtm, n_pairs):
    n_batch, seq, d = qb.shape
    q_spec, kv_spec, meta_spec = _attn_specs(0, seq)
    return pl.pallas_call(
        _sb_kernel,
        grid=(n_batch, n_pairs, seq // BLK),
        in_specs=[q_spec, kv_spec, kv_spec, meta_spec, meta_spec],
        out_specs=pl.BlockSpec((1, BLK, LANES), lambda b, p, i: (b, i, p)),
        out_shape=jax.ShapeDtypeStruct((n_batch, seq, n_pairs * LANES), F32),
        compiler_params=_params(("arbitrary", "arbitrary", "arbitrary")),
        name="sb_attn",
    )(qb, ktb, vtb, ktm, vtm)


FILL_COLS = 512


def _fox_kernel(q_ref, kt_ref, vt_ref, ktm_ref, vtm_ref, ck_ref, ckm_ref, o_ref, va_sc, vb_sc):
    i = pl.program_id(2)
    row = lax.broadcasted_iota(jnp.int32, (BLK, BLK), 0)
    col = lax.broadcasted_iota(jnp.int32, (BLK, BLK), 1)
    causal = col <= row
    meta_valid = col < N_META

    def augment(vt):
        vf = vt.astype(F32)
        top = lax.broadcasted_iota(jnp.int32, vf.shape, 0) < HEAD_DIM
        return jnp.where(top, vf, 1.0).astype(BF16), jnp.where(top, 1.0, vf).astype(BF16)

    @pl.when(i == 0)
    def _fill():
        def fill(c, carry):
            off = pl.multiple_of(c * FILL_COLS, FILL_COLS)
            va, vb = augment(vt_ref[0, :, pl.ds(off, FILL_COLS)])
            va_sc[:, pl.ds(off, FILL_COLS)] = va
            vb_sc[:, pl.ds(off, FILL_COLS)] = vb
            return carry
        lax.fori_loop(0, va_sc.shape[1] // FILL_COLS, fill, 0)

    q_heads = _split_heads(q_ref)

    def tile(qh, ktb, v_aug, ck_row, m, acc, mask):
        s = _dot(qh, ktb) - ck_row
        if mask is not None:
            s = jnp.where(mask, s, NEG)
        m_new = jnp.maximum(m, jnp.max(s, axis=-1, keepdims=True))
        p = jnp.exp(s - m_new)
        acc = jnp.exp(m - m_new) * acc + _dot_nt(p.astype(BF16), v_aug)
        return m_new, acc

    def step(ktb, va, vb, ck2, state, mask):
        m0, a0, m1, a1 = state
        m0, a0 = tile(q_heads[0], ktb, va, ck2[0:1, :], m0, a0, mask)
        m1, a1 = tile(q_heads[1], ktb, vb, ck2[1:2, :], m1, a1, mask)
        return m0, a0, m1, a1

    def main_step(off, state, mask):
        return step(kt_ref[0, :, pl.ds(off, BLK)], va_sc[:, pl.ds(off, BLK)], vb_sc[:, pl.ds(off, BLK)],
                    ck_ref[0, 0, :, pl.ds(off, BLK)], state, mask)

    m_init = jnp.full((BLK, 1), NEG, F32)
    zeros = jnp.zeros((BLK, LANES), F32)
    vma, vmb = augment(vtm_ref[...])
    state = step(ktm_ref[...], vma, vmb, ckm_ref[0], (m_init, zeros, m_init, zeros), meta_valid)
    state = lax.fori_loop(0, i, lambda j, st: main_step(pl.multiple_of(j * BLK, BLK), st, None), state)
    state = main_step(pl.multiple_of(i * BLK, BLK), state, causal)
    a0, a1 = state[1], state[3]
    o0 = a0 / pltpu.roll(a0, HEAD_DIM, 1)
    o1 = a1 / pltpu.roll(a1, HEAD_DIM, 1)
    low = lax.broadcasted_iota(jnp.int32, (BLK, LANES), 1) < HEAD_DIM
    o_ref[0] = jnp.where(low, o0, o1)


def _fox_call(qb, ktb, vtb, ktm, vtm, ck, ckm, n_pairs):
    n_batch, seq, d = qb.shape
    q_spec, kv_spec, meta_spec = _attn_specs(n_pairs, seq)
    return pl.pallas_call(
        _fox_kernel,
        grid=(n_batch, n_pairs, seq // BLK),
        in_specs=[q_spec, kv_spec, kv_spec, meta_spec, meta_spec,
                  pl.BlockSpec((1, 1, 2, seq), lambda b, p, i: (b, p, 0, 0)),
                  pl.BlockSpec((1, 2, LANES), lambda b, p, i: (p, 0, 0))],
        out_specs=pl.BlockSpec((1, BLK, LANES), lambda b, p, i: (b, i, p)),
        out_shape=jax.ShapeDtypeStruct((n_batch, seq, n_pairs * LANES), F32),
        scratch_shapes=[pltpu.VMEM((LANES, seq), BF16), pltpu.VMEM((LANES, seq), BF16)],
        compiler_params=_params(("arbitrary", "arbitrary", "arbitrary")),
        name="fox_attn",
    )(qb, ktb, vtb, ktm, vtm, ck, ckm)


PAGES_PER_STEP = 8


def _decode_kernel(pt_ref, q_ref, kn_ref, vn_ref, lfn_ref, *refs, n_pages, h_sb):
    pp = PAGES_PER_STEP
    k_refs = refs[:pp]
    v_refs = refs[pp:2 * pp]
    lf_refs = refs[2 * pp:3 * pp]
    o_ref = refs[3 * pp]
    qb_sc, z_sc, lf_sc, w_sc, t_sc, acc_sc = refs[3 * pp + 1:]
    ph = pl.program_id(1)
    j = pl.program_id(2)
    n_steps = n_pages // pp
    n_heads, hd, page = qb_sc.shape

    @pl.when((ph == 0) & (j == 0))
    def _spread_q():
        for h in range(n_heads):
            qb_sc[h] = jnp.broadcast_to(q_ref[0, h], (hd, page))

    @pl.when(ph == 0)
    def _scores():
        for s in range(pp):
            r = j * pp + s
            for h in range(n_heads):
                z_sc[r, pl.ds(h, 1), :] = jnp.sum(k_refs[s][0, 0, h] * qb_sc[h], axis=0, keepdims=True)
            lf_sc[r] = lf_refs[s][0, 0]

    @pl.when((ph == 0) & (j == n_steps - 1))
    def _weights():
        z = z_sc[...]
        z_sb = z[:, :h_sb, :]
        z_fx = z[:, h_sb:, :]
        sp = _softplus(z_sb)
        x = jnp.concatenate([sp, lf_sc[...]], axis=1).reshape(n_pages * n_heads, page)
        r = _dot_exact(x, _suffix_ones(page, F32))
        t_sc[...] = r[:, page:].reshape(n_pages, n_heads, page)

        def later_pages(n, carry):
            p = n_pages - 1 - n
            w_sc[p] = carry
            return carry + t_sc[p]

        lax.fori_loop(0, n_pages, later_pages, jnp.zeros((n_heads, page), F32))
        d = r[:, :page].reshape(n_pages, n_heads, page) + w_sc[...]
        a = jnp.exp((z_sb - sp) - d[:, :h_sb, :])
        logit = z_fx + d[:, h_sb:, :] + lfn_ref[0]
        zs = jnp.sum(q_ref[0] * kn_ref[0], axis=1)[h_sb:]
        m = jnp.maximum(jnp.max(jnp.max(logit, axis=0), axis=-1, keepdims=True), zs)
        pw = jnp.exp(logit - m)
        ps = jnp.exp(zs - m)
        den = jnp.sum(jnp.sum(pw, axis=0), axis=-1, keepdims=True) + ps
        w_sc[...] = jnp.concatenate([a, pw / den], axis=1)
        pself = ps / den
        first = lax.broadcasted_iota(jnp.int32, (hd, page), 1) == 0
        for h in range(n_heads):
            if h < h_sb:
                acc_sc[h] = jnp.zeros((hd, page), F32)
            else:
                acc_sc[h] = jnp.where(first, pself[h - h_sb:h - h_sb + 1, :] * vn_ref[0, h], 0.0)

    @pl.when(ph == 1)
    def _values():
        ws = [w_sc[j * pp + s] for s in range(pp)]
        for h in range(n_heads):
            tmp = None
            for s in range(pp):
                t = ws[s][h:h + 1, :] * v_refs[s][0, 0, h]
                tmp = t if tmp is None else tmp + t
            acc_sc[h] = acc_sc[h] + tmp

    @pl.when((ph == 1) & (j == n_steps - 1))
    def _finish():
        for h in range(n_heads):
            o_ref[0, h] = jnp.sum(acc_sc[h], axis=-1, keepdims=True)


def _decode_call(page_table, qcol, kcol, vcol, lfn, cache_kt, cache_vt, cache_lft, layer, h_sb):
    n_b, n_pages = page_table.shape
    _, _, n_heads, hd, page = cache_kt.shape
    h_fox = n_heads - h_sb
    pp = PAGES_PER_STEP
    n_steps = n_pages // pp

    def k_map(s):
        return lambda b, ph, j, pt: (layer, pt[b, jnp.where(ph == 0, j, n_steps - 1) * pp + s], 0, 0, 0)

    def v_map(s):
        return lambda b, ph, j, pt: (layer, pt[b, jnp.where(ph == 1, j, 0) * pp + s], 0, 0, 0)

    def lf_map(s):
        return lambda b, ph, j, pt: (layer, pt[b, jnp.where(ph == 0, j, n_steps - 1) * pp + s], 0, 0)

    col_spec = pl.BlockSpec((1, n_heads, hd, 1), lambda b, ph, j, pt: (b, 0, 0, 0))
    page_block = (1, 1, n_heads, hd, page)
    grid_spec = pltpu.PrefetchScalarGridSpec(
        num_scalar_prefetch=1,
        grid=(n_b, 2, n_steps),
        in_specs=[col_spec, col_spec, col_spec, pl.BlockSpec((1, h_fox, 1), lambda b, ph, j, pt: (b, 0, 0))]
                 + [pl.BlockSpec(page_block, k_map(s)) for s in range(pp)]
                 + [pl.BlockSpec(page_block, v_map(s)) for s in range(pp)]
                 + [pl.BlockSpec((1, 1, h_fox, page), lf_map(s)) for s in range(pp)],
        out_specs=col_spec,
        scratch_shapes=[pltpu.VMEM((n_heads, hd, page), F32),
                        pltpu.VMEM((n_pages, n_heads, page), F32),
                        pltpu.VMEM((n_pages, h_fox, page), F32),
                        pltpu.VMEM((n_pages, n_heads, page), F32),
                        pltpu.VMEM((n_pages, n_heads, page), F32),
                        pltpu.VMEM((n_heads, hd, page), F32)],
    )
    return pl.pallas_call(
        functools.partial(_decode_kernel, n_pages=n_pages, h_sb=h_sb),
        grid_spec=grid_spec,
        out_shape=jax.ShapeDtypeStruct((n_b, n_heads, hd, 1), F32),
        compiler_params=_params(("arbitrary", "arbitrary", "arbitrary")),
        name="decode_attn",
    )(page_table, qcol, kcol, vcol, lfn, *([cache_kt] * pp), *([cache_vt] * pp), *([cache_lft] * pp))


FF_CHUNK = 1024


def _mlp_kernel(osb_ref, ofx_ref, x_ref, gcat_ref, wout_ref, gm_ref, wup_ref, wdn_ref, y_ref):
    def norm(t):
        return t * lax.rsqrt(jnp.mean(t * t, axis=-1, keepdims=True) + EPS)

    on = jnp.concatenate([norm(osb_ref[...]), norm(ofx_ref[...])], axis=1) * gcat_ref[...]
    x1 = x_ref[...] + _dot(on.astype(BF16), wout_ref[...])
    h = (norm(x1) * gm_ref[...]).astype(BF16)
    acc = jnp.zeros_like(x1)
    for c in range(wup_ref.shape[1] // FF_CHUNK):
        u = jnp.maximum(_dot(h, wup_ref[:, c * FF_CHUNK:(c + 1) * FF_CHUNK]), 0.0)
        acc = acc + _dot((u * u).astype(BF16), wdn_ref[c * FF_CHUNK:(c + 1) * FF_CHUNK, :])
    y_ref[...] = x1 + acc


def _mlp_call(o_sb, o_fx, x2d, tm, gcat, wout, gm, wup, wdn):
    rows, d = x2d.shape
    row_spec = pl.BlockSpec((tm, d), lambda i: (i, 0))
    half_spec = pl.BlockSpec((tm, d // 2), lambda i: (i, 0))
    return pl.pallas_call(
        _mlp_kernel,
        grid=(rows // tm,),
        in_specs=[half_spec, half_spec, row_spec, _const_spec(gcat.shape), _const_spec(wout.shape),
                  _const_spec(gm.shape), _const_spec(wup.shape), _const_spec(wdn.shape)],
        out_specs=row_spec,
        out_shape=jax.ShapeDtypeStruct((rows, d), F32),
        compiler_params=_params(("arbitrary",)),
        name="merge_mlp",
    )(o_sb, o_fx, x2d, gcat, wout, gm, wup, wdn)


def kernel(x_prompt, x_sample, cache_k, cache_v, cache_logf, page_table, meta_tokens, norm_attn, w_in, b_forget, q_norm, k_norm, out_norm_sb, out_norm_fox, w_out, norm_mlp, w_up, w_down):
    n_batch, seq, d = x_prompt.shape
    n_dec = x_sample.shape[0]
    depth = w_in.shape[0]
    w_sb = d // 2
    h_fox = b_forget.shape[1]
    h_sb = w_sb // HEAD_DIM
    n_heads = h_sb + h_fox
    n_pairs = w_sb // LANES

    r = lax.broadcasted_iota(jnp.int32, (w_sb, w_sb), 0) // HEAD_DIM
    c = lax.broadcasted_iota(jnp.int32, (w_sb, w_sb), 1) // HEAD_DIM
    grp = (r == c).astype(BF16)

    cache_kt = jnp.transpose(cache_k, (0, 1, 3, 4, 2))
    cache_vt = jnp.transpose(cache_v, (0, 1, 3, 4, 2))
    cache_lft = jnp.transpose(cache_logf, (0, 1, 3, 2))

    xp = x_prompt.reshape(n_batch * seq, d)
    xs = x_sample.reshape(n_dec, d)
    pad_rows = lambda a: jnp.pad(a, ((0, BLK - a.shape[0]), (0, 0)))
    outs = [[] for _ in range(6)]
    for l in range(depth):
        wt = w_in[l].T
        rows = lambda a, b: wt[a:b]
        wq = jnp.concatenate([rows(0, w_sb), rows(3 * w_sb, 4 * w_sb)], axis=0).T.astype(BF16)
        wkvf = jnp.concatenate([rows(w_sb, 2 * w_sb), rows(4 * w_sb, 5 * w_sb),
                                rows(2 * w_sb, 3 * w_sb), rows(5 * w_sb, 6 * w_sb),
                                rows(6 * w_sb, 6 * w_sb + h_fox),
                                jnp.zeros((16 - h_fox, d), w_in.dtype)], axis=0).astype(BF16)
        bf = b_forget[l].reshape(h_fox, 1)
        ga = norm_attn[l].reshape(1, d)
        gq = jnp.tile(q_norm[l], h_fox).reshape(1, w_sb)
        gk = jnp.tile(k_norm[l], h_fox).reshape(w_sb, 1)
        proj = lambda x2d, nb, tm: _proj_call(x2d, nb, tm, ga, wq, wkvf, bf, gq, gk, grp)
        gcat = jnp.concatenate([out_norm_sb[l], out_norm_fox[l]]).reshape(1, d)
        gm = norm_mlp[l].reshape(1, d)
        wo = w_out[l].astype(BF16)
        wu = w_up[l].astype(BF16)
        wd = w_down[l].astype(BF16)

        _, ktm, vtm, kt32_m, vt32_m, lft_m = proj(pad_rows(meta_tokens.astype(x_prompt.dtype)), 1, BLK)
        qb, ktb, vtb, kt32, vt32, lft = proj(xp, n_batch, 512)
        lft_meta = jnp.where(jnp.arange(BLK)[None, :] < N_META, lft_m[0], 0.0)
        c_main, c_meta = _cumsum_call(lft, lft_meta)
        qb3 = qb.reshape(n_batch, seq, d)
        o_sb = _sb_call(qb3, ktb, vtb, ktm[0], vtm[0], n_pairs)
        o_fx = _fox_call(qb3, ktb, vtb, ktm[0], vtm[0], c_main.reshape(n_batch, n_pairs, 2, seq),
                         c_meta.reshape(n_pairs, 2, LANES), n_pairs)
        xp = _mlp_call(o_sb.reshape(n_batch * seq, w_sb), o_fx.reshape(n_batch * seq, w_sb), xp, 256,
                       gcat, wo, gm, wu, wd)

        def with_meta(main, meta):
            m = jnp.broadcast_to(meta[:, :, :N_META], (n_batch,) + meta.shape[1:2] + (N_META,))
            return jnp.concatenate([m, main], axis=-1)

        tok_major = lambda a: jnp.transpose(a.reshape(n_batch, n_heads, HEAD_DIM, N_META + seq), (0, 3, 1, 2))
        outs[0].append(tok_major(with_meta(kt32, kt32_m)))
        outs[1].append(tok_major(with_meta(vt32, vt32_m)))
        outs[2].append(jnp.transpose(with_meta(lft, lft_m), (0, 2, 1)))

        qb_s, _, _, kt32_s, vt32_s, lft_s = proj(pad_rows(xs), 1, BLK)
        k_s = kt32_s[0, :, :n_dec].T.reshape(n_dec, n_heads, HEAD_DIM)
        v_s = vt32_s[0, :, :n_dec].T.reshape(n_dec, n_heads, HEAD_DIM)
        lf_s = lft_s[0, :, :n_dec].T
        qcol = qb_s[:n_dec].astype(F32).reshape(n_dec, n_heads, HEAD_DIM, 1)
        o_s = _decode_call(page_table, qcol, k_s[..., None], v_s[..., None], lf_s[..., None],
                           cache_kt, cache_vt, cache_lft, l, h_sb)
        o_s = o_s.reshape(n_dec, d)
        xs = _mlp_call(o_s[:, :w_sb], o_s[:, w_sb:], xs, n_dec, gcat, wo, gm, wu, wd)
        outs[3].append(k_s.reshape(n_dec, 1, n_heads, HEAD_DIM))
        outs[4].append(v_s.reshape(n_dec, 1, n_heads, HEAD_DIM))
        outs[5].append(lf_s.reshape(n_dec, 1, h_fox))

    y_prompt = xp.reshape(n_batch, seq, d)
    y_sample = xs.reshape(n_dec, 1, d)
    return (y_prompt, y_sample) + tuple(jnp.stack(o) for o in outs)
```

```python
import functools
import math

import jax
import jax.numpy as jnp
from jax import lax
from jax.experimental import pallas as pl
from jax.experimental.pallas import tpu as pltpu

F32 = jnp.float32
BF16 = jnp.bfloat16

HEAD_DIM = 64
N_META = 16
EPS = 1e-6
NEG = -1e30
LANES = 128
BLK = 128
VMEM_LIMIT = 56 * 1024 * 1024


def _params(sem):
    return pltpu.CompilerParams(dimension_semantics=sem, vmem_limit_bytes=VMEM_LIMIT)


def _dot(a, b):
    return jnp.dot(a, b, preferred_element_type=F32)


def _dot_nt(a, b):
    return lax.dot_general(a, b, (((1,), (1,)), ((), ())), preferred_element_type=F32)


def _dot_exact(a, b):
    return jnp.dot(a, b, preferred_element_type=F32, precision=lax.Precision.HIGHEST)


LOG2E = math.log2(math.e)
Q_SCALE = LOG2E / math.sqrt(HEAD_DIM)


def _softplus2(u):
    neg_abs = lax.bitcast_convert_type(
        lax.bitcast_convert_type(u, jnp.uint32) | jnp.uint32(0x80000000), F32)
    return jnp.maximum(u, 0.0) + jnp.log2(1.0 + jnp.exp2(neg_abs))


def _const_spec(shape):
    nd = len(shape)
    return pl.BlockSpec(shape, lambda *_: (0,) * nd, pipeline_mode=pl.Buffered(1))


def _suffix_ones(n, dtype):
    r = lax.broadcasted_iota(jnp.int32, (n, 2 * n), 0)
    c = lax.broadcasted_iota(jnp.int32, (n, 2 * n), 1)
    return jnp.where((r > c) | (c >= n), 1.0, 0.0).astype(dtype)


def _proj_kernel(x_ref, ga_ref, wq_ref, wkvf_ref, bf_ref, gq_ref, gk_ref, grp_ref,
                 qb_ref, ktb_ref, vtb_ref, kt32_ref, vt32_ref, lft_ref):
    x = x_ref[...]
    d = x.shape[1]
    w_fox = d // 2
    n_fox = w_fox // HEAD_DIM
    tm = x.shape[0]
    inv = lax.rsqrt(jnp.mean(x * x, axis=-1, keepdims=True) + EPS)
    h = (x * inv * ga_ref[...]).astype(BF16)
    scale = Q_SCALE

    q = _dot(h, wq_ref[...])
    q_fx = q[:, w_fox:]
    sq = q_fx * q_fx
    hi = sq.astype(BF16)
    lo = (sq - hi.astype(F32)).astype(BF16)
    grp = grp_ref[...]
    ms = (_dot(hi, grp) + _dot(lo, grp)) * (1.0 / HEAD_DIM)
    q_fx = q_fx * lax.rsqrt(ms + EPS) * gq_ref[...] * scale
    qb_ref[...] = jnp.concatenate([q[:, :w_fox] * scale, q_fx], axis=1).astype(BF16)

    kvf = _dot_nt(wkvf_ref[...], h)
    k_fx = kvf[w_fox:d].reshape(n_fox, HEAD_DIM, tm)
    k_fx = k_fx * lax.rsqrt(jnp.mean(k_fx * k_fx, axis=1, keepdims=True) + EPS)
    k_fx = k_fx.reshape(w_fox, tm) * gk_ref[...]
    kt = jnp.concatenate([kvf[:w_fox], k_fx], axis=0)
    vt = kvf[d:2 * d]
    ktb_ref[0] = kt.astype(BF16)
    vtb_ref[0] = vt.astype(BF16)
    kt32_ref[0] = kt
    vt32_ref[0] = vt

    y = kvf[2 * d:2 * d + n_fox] + bf_ref[...]
    lft_ref[0] = jnp.minimum(y, 0.0) - jnp.log(1.0 + jnp.exp(-jnp.abs(y)))


def _proj_call(x2d, n_batch, tm, ga, wq, wkvf, bf, gq, gk, grp):
    rows, d = x2d.shape
    seq = rows // n_batch
    per = seq // tm
    n_f = bf.shape[0]
    row_spec = pl.BlockSpec((tm, d), lambda i: (i, 0))
    t_spec = lambda r: pl.BlockSpec((1, r, tm), lambda i: (i // per, 0, i % per))
    t_shape = lambda r, dt: jax.ShapeDtypeStruct((n_batch, r, seq), dt)
    return pl.pallas_call(
        _proj_kernel,
        grid=(rows // tm,),
        in_specs=[row_spec] + [_const_spec(a.shape) for a in (ga, wq, wkvf, bf, gq, gk, grp)],
        out_specs=[row_spec, t_spec(d), t_spec(d), t_spec(d), t_spec(d), t_spec(n_f)],
        out_shape=[jax.ShapeDtypeStruct((rows, d), BF16), t_shape(d, BF16), t_shape(d, BF16),
                   t_shape(d, F32), t_shape(d, F32), t_shape(n_f, F32)],
        compiler_params=_params(("arbitrary",)),
        name="proj",
    )(x2d, ga, wq, wkvf, bf, gq, gk, grp)


def _cumsum_kernel(lf_ref, lfmeta_ref, c_ref, cmeta_ref):
    n_batch, n_h, seq = lf_ref.shape
    r = lax.broadcasted_iota(jnp.int32, (LANES, LANES), 0)
    c = lax.broadcasted_iota(jnp.int32, (LANES, LANES), 1)
    tri = (r <= c).astype(F32)
    cmeta = _dot_exact(lfmeta_ref[...], tri)
    cmeta_ref[...] = cmeta
    base = jnp.broadcast_to(cmeta[:, LANES - 1:LANES], (n_h, LANES))
    for b in range(n_batch):
        def body(i, carry):
            off = pl.multiple_of(i * LANES, LANES)
            pre = _dot_exact(lf_ref[b, :, pl.ds(off, LANES)], tri) + carry
            c_ref[b, :, pl.ds(off, LANES)] = pre
            return jnp.broadcast_to(pre[:, LANES - 1:LANES], (n_h, LANES))
        lax.fori_loop(0, seq // LANES, body, base)


def _cumsum_call(lft_main, lft_meta):
    return pl.pallas_call(
        _cumsum_kernel,
        out_shape=[jax.ShapeDtypeStruct(lft_main.shape, F32), jax.ShapeDtypeStruct(lft_meta.shape, F32)],
        compiler_params=pltpu.CompilerParams(vmem_limit_bytes=VMEM_LIMIT),
        name="cumsum",
    )(lft_main, lft_meta)


TK = 256
TILES = 2
TQ = TILES * TK


def _diag_mask(tile, inclusive):
    row = lax.broadcasted_iota(jnp.int32, (TQ, TK), 0)
    col = lax.broadcasted_iota(jnp.int32, (TQ, TK), 1) + tile * TK
    return (col <= row) if inclusive else (col < row)


def _attn_specs(pair_offset, seq):
    q_spec = pl.BlockSpec((1, TQ, LANES), lambda b, p, i: (b, i, p + pair_offset))
    kv_spec = pl.BlockSpec((1, LANES, seq), lambda b, p, i: (b, p + pair_offset, 0))
    meta_spec = pl.BlockSpec((LANES, BLK), lambda b, p, i: (p + pair_offset, 0))
    return q_spec, kv_spec, meta_spec


def _split_heads(q_ref):
    q = q_ref[0].astype(F32)
    low = lax.broadcasted_iota(jnp.int32, q.shape, 1) < HEAD_DIM
    return jnp.where(low, q, 0.0).astype(BF16), jnp.where(low, 0.0, q).astype(BF16)


def _sb_kernel(q_ref, kt_ref, vt_ref, ktm_ref, vtm_ref, o_ref, carry_sc, acc_sc):
    i = pl.program_id(2)
    suffix_ones = _suffix_ones(BLK, BF16)
    q_heads = _split_heads(q_ref)
    carry_sc[...] = jnp.zeros_like(carry_sc)
    acc_sc[...] = jnp.zeros_like(acc_sc)

    def group(tiles):
        work = [(h, kt, vt, mask) for kt, vt, mask in tiles for h in (0, 1)]
        zs = [_dot(q_heads[h], kt) for h, kt, _, _ in work]
        sps = []
        for z, (_, _, _, mask) in zip(zs, work):
            sp = _softplus2(z)
            sps.append(sp if mask is None else jnp.where(mask, sp, 0.0))
        spbs = [sp.astype(BF16) for sp in sps]
        rs = [[_dot(spb[:, c * BLK:(c + 1) * BLK], suffix_ones) for c in range(spb.shape[1] // BLK)]
              for spb in spbs]
        run = [carry_sc[0], carry_sc[1]]
        probs = []
        for z, sp, r, (h, _, _, mask) in zip(zs, sps, rs, work):
            after = [None] * len(r)
            for c in reversed(range(len(r))):
                after[c] = r[c][:, :BLK] + run[h]
                run[h] = run[h] + r[c][:, BLK:]
            arg = (z - sp) - jnp.concatenate(after, axis=1)
            if mask is not None:
                arg = jnp.where(mask, arg, NEG)
            probs.append(jnp.exp2(arg).astype(BF16))
        pvs = [_dot_nt(p, vt) for p, (_, _, vt, _) in zip(probs, work)]
        for h in (0, 1):
            acc_sc[h] = acc_sc[h] + sum(pv for pv, w in zip(pvs, work) if w[0] == h)
            carry_sc[h] = run[h]

    def main_tile(off, mask):
        return kt_ref[0, :, pl.ds(off, TK)], vt_ref[0, :, pl.ds(off, TK)], mask

    def tile_group(g, diagonal):
        base = g * TQ
        return [main_tile(pl.multiple_of(base + t * TK, TK), _diag_mask(t, False) if diagonal else None)
                for t in reversed(range(TILES))]

    group(tile_group(i, True))

    def body(j, carry):
        group(tile_group(i - 1 - j, False))
        return carry

    lax.fori_loop(0, i, body, 0)
    group([(ktm_ref[...], vtm_ref[...], lax.broadcasted_iota(jnp.int32, (TQ, BLK), 1) < N_META)])
    low = lax.broadcasted_iota(jnp.int32, (TQ, LANES), 1) < HEAD_DIM
    o_ref[0] = jnp.where(low, acc_sc[0], acc_sc[1])


def _sb_call(qb, ktb, vtb, ktm, vtm, n_pairs):
    n_batch, seq, d = qb.shape
    q_spec, kv_spec, meta_spec = _attn_specs(0, seq)
    return pl.pallas_call(
        _sb_kernel,
        grid=(n_batch, n_pairs, seq // TQ),
        in_specs=[q_spec, kv_spec, kv_spec, meta_spec, meta_spec],
        out_specs=pl.BlockSpec((1, TQ, LANES), lambda b, p, i: (b, i, p)),
        out_shape=jax.ShapeDtypeStruct((n_batch, seq, n_pairs * LANES), F32),
        scratch_shapes=[pltpu.VMEM((2, TQ, LANES), F32), pltpu.VMEM((2, TQ, LANES), F32)],
        compiler_params=_params(("arbitrary", "arbitrary", "arbitrary")),
        name="sb_attn",
    )(qb, ktb, vtb, ktm, vtm)


FILL_COLS = 512


def _fox_kernel(q_ref, kt_ref, vt_ref, ktm_ref, vtm_ref, ck_ref, ckm_ref, o_ref, va_sc, vb_sc, m_sc, acc_sc):
    i = pl.program_id(2)

    def augment(vt):
        vf = vt.astype(F32)
        top = lax.broadcasted_iota(jnp.int32, vf.shape, 0) < HEAD_DIM
        return jnp.where(top, vf, 1.0).astype(BF16), jnp.where(top, 1.0, vf).astype(BF16)

    @pl.when(i == 0)
    def _fill():
        def fill(c, carry):
            off = pl.multiple_of(c * FILL_COLS, FILL_COLS)
            va, vb = augment(vt_ref[0, :, pl.ds(off, FILL_COLS)])
            va_sc[:, pl.ds(off, FILL_COLS)] = va
            vb_sc[:, pl.ds(off, FILL_COLS)] = vb
            return carry
        lax.fori_loop(0, va_sc.shape[1] // FILL_COLS, fill, 0)

    q_heads = _split_heads(q_ref)
    m_sc[...] = jnp.full(m_sc.shape, NEG, F32)
    acc_sc[...] = jnp.zeros_like(acc_sc)

    def group(tiles):
        work = [(h, kt, vs[h], ck2[h:h + 1, :] * LOG2E, mask) for kt, vs, ck2, mask in tiles for h in (0, 1)]
        ss = []
        for h, kt, _, ck_row, mask in work:
            s = _dot(q_heads[h], kt) - ck_row
            ss.append(s if mask is None else jnp.where(mask, s, NEG))
        m_new, alpha = [], []
        for h in (0, 1):
            blocks = [s[:, c * BLK:(c + 1) * BLK] for s, w in zip(ss, work) if w[0] == h
                      for c in range(s.shape[1] // BLK)]
            folded = functools.reduce(jnp.maximum, blocks)
            m = m_sc[h]
            m_new.append(jnp.maximum(m, jnp.max(folded, axis=-1, keepdims=True)))
            alpha.append(jnp.exp2(m - m_new[h]))
        ps = [jnp.exp2(s - m_new[w[0]]).astype(BF16) for s, w in zip(ss, work)]
        pvs = [_dot_nt(p, w[2]) for p, w in zip(ps, work)]
        for h in (0, 1):
            acc_sc[h] = alpha[h] * acc_sc[h] + sum(pv for pv, w in zip(pvs, work) if w[0] == h)
            m_sc[h] = m_new[h]

    def main_tile(off, mask):
        return (kt_ref[0, :, pl.ds(off, TK)], (va_sc[:, pl.ds(off, TK)], vb_sc[:, pl.ds(off, TK)]),
                ck_ref[0, 0, :, pl.ds(off, TK)], mask)

    group([(ktm_ref[...], augment(vtm_ref[...]), ckm_ref[0],
            lax.broadcasted_iota(jnp.int32, (TQ, BLK), 1) < N_META)])

    def tile_group(g, diagonal):
        base = g * TQ
        return [main_tile(pl.multiple_of(base + t * TK, TK), _diag_mask(t, True) if diagonal else None)
                for t in range(TILES)]

    def body(j, carry):
        group(tile_group(j, False))
        return carry

    lax.fori_loop(0, i, body, 0)
    group(tile_group(i, True))
    a0, a1 = acc_sc[0], acc_sc[1]
    o0 = a0 / pltpu.roll(a0, HEAD_DIM, 1)
    o1 = a1 / pltpu.roll(a1, HEAD_DIM, 1)
    low = lax.broadcasted_iota(jnp.int32, (TQ, LANES), 1) < HEAD_DIM
    o_ref[0] = jnp.where(low, o0, o1)


def _fox_call(qb, ktb, vtb, ktm, vtm, ck, ckm, n_pairs):
    n_batch, seq, d = qb.shape
    q_spec, kv_spec, meta_spec = _attn_specs(n_pairs, seq)
    return pl.pallas_call(
        _fox_kernel,
        grid=(n_batch, n_pairs, seq // TQ),
        in_specs=[q_spec, kv_spec, kv_spec, meta_spec, meta_spec,
                  pl.BlockSpec((1, 1, 2, seq), lambda b, p, i: (b, p, 0, 0)),
                  pl.BlockSpec((1, 2, LANES), lambda b, p, i: (p, 0, 0))],
        out_specs=pl.BlockSpec((1, TQ, LANES), lambda b, p, i: (b, i, p)),
        out_shape=jax.ShapeDtypeStruct((n_batch, seq, n_pairs * LANES), F32),
        scratch_shapes=[pltpu.VMEM((LANES, seq), BF16), pltpu.VMEM((LANES, seq), BF16),
                        pltpu.VMEM((2, TQ, 1), F32), pltpu.VMEM((2, TQ, LANES), F32)],
        compiler_params=_params(("arbitrary", "arbitrary", "arbitrary")),
        name="fox_attn",
    )(qb, ktb, vtb, ktm, vtm, ck, ckm)


PAGES_PER_STEP = 8


def _decode_kernel(pt_ref, q_ref, kn_ref, vn_ref, lfn_ref, *refs, n_pages, h_sb):
    pp = PAGES_PER_STEP
    k_refs = refs[:pp]
    v_refs = refs[pp:2 * pp]
    lf_refs = refs[2 * pp:3 * pp]
    o_ref = refs[3 * pp]
    qb_sc, z_sc, lf_sc, w_sc, t_sc, acc_sc = refs[3 * pp + 1:]
    ph = pl.program_id(1)
    j = pl.program_id(2)
    n_steps = n_pages // pp
    n_heads, hd, page = qb_sc.shape

    @pl.when((ph == 0) & (j == 0))
    def _spread_q():
        for h in range(n_heads):
            qb_sc[h] = jnp.broadcast_to(q_ref[0, h], (hd, page))

    @pl.when(ph == 0)
    def _scores():
        for s in range(pp):
            r = j * pp + s
            for h in range(n_heads):
                z_sc[r, pl.ds(h, 1), :] = jnp.sum(k_refs[s][0, 0, h] * qb_sc[h], axis=0, keepdims=True)
            lf_sc[r] = lf_refs[s][0, 0]

    @pl.when((ph == 0) & (j == n_steps - 1))
    def _weights():
        z = z_sc[...]
        z_sb = z[:, :h_sb, :]
        z_fx = z[:, h_sb:, :]
        sp = _softplus2(z_sb)
        x = jnp.concatenate([sp, lf_sc[...] * LOG2E], axis=1).reshape(n_pages * n_heads, page)
        r = _dot_exact(x, _suffix_ones(page, F32))
        t_sc[...] = r[:, page:].reshape(n_pages, n_heads, page)

        def later_pages(n, carry):
            p = n_pages - 1 - n
            w_sc[p] = carry
            return carry + t_sc[p]

        lax.fori_loop(0, n_pages, later_pages, jnp.zeros((n_heads, page), F32))
        d = r[:, :page].reshape(n_pages, n_heads, page) + w_sc[...]
        a = jnp.exp2((z_sb - sp) - d[:, :h_sb, :])
        logit = z_fx + d[:, h_sb:, :] + lfn_ref[0] * LOG2E
        zs = jnp.sum(q_ref[0] * kn_ref[0], axis=1)[h_sb:]
        m = jnp.maximum(jnp.max(jnp.max(logit, axis=0), axis=-1, keepdims=True), zs)
        pw = jnp.exp2(logit - m)
        ps = jnp.exp2(zs - m)
        den = jnp.sum(jnp.sum(pw, axis=0), axis=-1, keepdims=True) + ps
        w_sc[...] = jnp.concatenate([a, pw / den], axis=1)
        pself = ps / den
        first = lax.broadcasted_iota(jnp.int32, (hd, page), 1) == 0
        for h in range(n_heads):
            if h < h_sb:
                acc_sc[h] = jnp.zeros((hd, page), F32)
            else:
                acc_sc[h] = jnp.where(first, pself[h - h_sb:h - h_sb + 1, :] * vn_ref[0, h], 0.0)

    @pl.when(ph == 1)
    def _values():
        ws = [w_sc[j * pp + s] for s in range(pp)]
        for h in range(n_heads):
            tmp = None
            for s in range(pp):
                t = ws[s][h:h + 1, :] * v_refs[s][0, 0, h]
                tmp = t if tmp is None else tmp + t
            acc_sc[h] = acc_sc[h] + tmp

    @pl.when((ph == 1) & (j == n_steps - 1))
    def _finish():
        for h in range(n_heads):
            o_ref[0, h] = jnp.sum(acc_sc[h], axis=-1, keepdims=True)


def _decode_call(page_table, qcol, kcol, vcol, lfn, cache_kt, cache_vt, cache_lft, layer, h_sb):
    n_b, n_pages = page_table.shape
    _, _, n_heads, hd, page = cache_kt.shape
    h_fox = n_heads - h_sb
    pp = PAGES_PER_STEP
    n_steps = n_pages // pp

    def k_map(s):
        return lambda b, ph, j, pt: (layer, pt[b, jnp.where(ph == 0, j, n_steps - 1) * pp + s], 0, 0, 0)

    def v_map(s):
        return lambda b, ph, j, pt: (layer, pt[b, jnp.where(ph == 1, j, 0) * pp + s], 0, 0, 0)

    def lf_map(s):
        return lambda b, ph, j, pt: (layer, pt[b, jnp.where(ph == 0, j, n_steps - 1) * pp + s], 0, 0)

    col_spec = pl.BlockSpec((1, n_heads, hd, 1), lambda b, ph, j, pt: (b, 0, 0, 0))
    page_block = (1, 1, n_heads, hd, page)
    grid_spec = pltpu.PrefetchScalarGridSpec(
        num_scalar_prefetch=1,
        grid=(n_b, 2, n_steps),
        in_specs=[col_spec, col_spec, col_spec, pl.BlockSpec((1, h_fox, 1), lambda b, ph, j, pt: (b, 0, 0))]
                 + [pl.BlockSpec(page_block, k_map(s)) for s in range(pp)]
                 + [pl.BlockSpec(page_block, v_map(s)) for s in range(pp)]
                 + [pl.BlockSpec((1, 1, h_fox, page), lf_map(s)) for s in range(pp)],
        out_specs=col_spec,
        scratch_shapes=[pltpu.VMEM((n_heads, hd, page), F32),
                        pltpu.VMEM((n_pages, n_heads, page), F32),
                        pltpu.VMEM((n_pages, h_fox, page), F32),
                        pltpu.VMEM((n_pages, n_heads, page), F32),
                        pltpu.VMEM((n_pages, n_heads, page), F32),
                        pltpu.VMEM((n_heads, hd, page), F32)],
    )
    return pl.pallas_call(
        functools.partial(_decode_kernel, n_pages=n_pages, h_sb=h_sb),
        grid_spec=grid_spec,
        out_shape=jax.ShapeDtypeStruct((n_b, n_heads, hd, 1), F32),
        compiler_params=_params(("arbitrary", "arbitrary", "arbitrary")),
        name="decode_attn",
    )(page_table, qcol, kcol, vcol, lfn, *([cache_kt] * pp), *([cache_vt] * pp), *([cache_lft] * pp))


FF_CHUNK = 1024


def _mlp_kernel(osb_ref, ofx_ref, x_ref, gcat_ref, wout_ref, gm_ref, wup_ref, wdn_ref, y_ref):
    def norm(t):
        return t * lax.rsqrt(jnp.mean(t * t, axis=-1, keepdims=True) + EPS)

    on = jnp.concatenate([norm(osb_ref[...]), norm(ofx_ref[...])], axis=1) * gcat_ref[...]
    x1 = x_ref[...] + _dot(on.astype(BF16), wout_ref[...])
    h = (norm(x1) * gm_ref[...]).astype(BF16)
    acc = jnp.zeros_like(x1)
    for c in range(wup_ref.shape[1] // FF_CHUNK):
        u = jnp.maximum(_dot(h, wup_ref[:, c * FF_CHUNK:(c + 1) * FF_CHUNK]), 0.0)
        acc = acc + _dot((u * u).astype(BF16), wdn_ref[c * FF_CHUNK:(c + 1) * FF_CHUNK, :])
    y_ref[...] = x1 + acc


def _mlp_call(o_sb, o_fx, x2d, tm, gcat, wout, gm, wup, wdn):
    rows, d = x2d.shape
    row_spec = pl.BlockSpec((tm, d), lambda i: (i, 0))
    half_spec = pl.BlockSpec((tm, d // 2), lambda i: (i, 0))
    return pl.pallas_call(
        _mlp_kernel,
        grid=(rows // tm,),
        in_specs=[half_spec, half_spec, row_spec, _const_spec(gcat.shape), _const_spec(wout.shape),
                  _const_spec(gm.shape), _const_spec(wup.shape), _const_spec(wdn.shape)],
        out_specs=row_spec,
        out_shape=jax.ShapeDtypeStruct((rows, d), F32),
        compiler_params=_params(("arbitrary",)),
        name="merge_mlp",
    )(o_sb, o_fx, x2d, gcat, wout, gm, wup, wdn)


def kernel(x_prompt, x_sample, cache_k, cache_v, cache_logf, page_table, meta_tokens, norm_attn, w_in, b_forget, q_norm, k_norm, out_norm_sb, out_norm_fox, w_out, norm_mlp, w_up, w_down):
    n_batch, seq, d = x_prompt.shape
    n_dec = x_sample.shape[0]
    depth = w_in.shape[0]
    w_sb = d // 2
    h_fox = b_forget.shape[1]
    h_sb = w_sb // HEAD_DIM
    n_heads = h_sb + h_fox
    n_pairs = w_sb // LANES

    r = lax.broadcasted_iota(jnp.int32, (w_sb, w_sb), 0) // HEAD_DIM
    c = lax.broadcasted_iota(jnp.int32, (w_sb, w_sb), 1) // HEAD_DIM
    grp = (r == c).astype(BF16)

    cache_kt = jnp.transpose(cache_k, (0, 1, 3, 4, 2))
    cache_vt = jnp.transpose(cache_v, (0, 1, 3, 4, 2))
    cache_lft = jnp.transpose(cache_logf, (0, 1, 3, 2))

    xp = x_prompt.reshape(n_batch * seq, d)
    xs = x_sample.reshape(n_dec, d)
    pad_rows = lambda a: jnp.pad(a, ((0, BLK - a.shape[0]), (0, 0)))
    outs = [[] for _ in range(6)]
    for l in range(depth):
        wt = w_in[l].T
        rows = lambda a, b: wt[a:b]
        wq = jnp.concatenate([rows(0, w_sb), rows(3 * w_sb, 4 * w_sb)], axis=0).T.astype(BF16)
        wkvf = jnp.concatenate([rows(w_sb, 2 * w_sb), rows(4 * w_sb, 5 * w_sb),
                                rows(2 * w_sb, 3 * w_sb), rows(5 * w_sb, 6 * w_sb),
                                rows(6 * w_sb, 6 * w_sb + h_fox),
                                jnp.zeros((16 - h_fox, d), w_in.dtype)], axis=0).astype(BF16)
        bf = b_forget[l].reshape(h_fox, 1)
        ga = norm_attn[l].reshape(1, d)
        gq = jnp.tile(q_norm[l], h_fox).reshape(1, w_sb)
        gk = jnp.tile(k_norm[l], h_fox).reshape(w_sb, 1)
        proj = lambda x2d, nb, tm: _proj_call(x2d, nb, tm, ga, wq, wkvf, bf, gq, gk, grp)
        gcat = jnp.concatenate([out_norm_sb[l], out_norm_fox[l]]).reshape(1, d)
        gm = norm_mlp[l].reshape(1, d)
        wo = w_out[l].astype(BF16)
        wu = w_up[l].astype(BF16)
        wd = w_down[l].astype(BF16)

        _, ktm, vtm, kt32_m, vt32_m, lft_m = proj(pad_rows(meta_tokens.astype(x_prompt.dtype)), 1, BLK)
        qb, ktb, vtb, kt32, vt32, lft = proj(xp, n_batch, 512)
        lft_meta = jnp.where(jnp.arange(BLK)[None, :] < N_META, lft_m[0], 0.0)
        c_main, c_meta = _cumsum_call(lft, lft_meta)
        qb3 = qb.reshape(n_batch, seq, d)
        o_sb = _sb_call(qb3, ktb, vtb, ktm[0], vtm[0], n_pairs)
        o_fx = _fox_call(qb3, ktb, vtb, ktm[0], vtm[0], c_main.reshape(n_batch, n_pairs, 2, seq),
                         c_meta.reshape(n_pairs, 2, LANES), n_pairs)
        xp = _mlp_call(o_sb.reshape(n_batch * seq, w_sb), o_fx.reshape(n_batch * seq, w_sb), xp, 256,
                       gcat, wo, gm, wu, wd)

        def with_meta(main, meta):
            m = jnp.broadcast_to(meta[:, :, :N_META], (n_batch,) + meta.shape[1:2] + (N_META,))
            return jnp.concatenate([m, main], axis=-1)

        tok_major = lambda a: jnp.transpose(a.reshape(n_batch, n_heads, HEAD_DIM, N_META + seq), (0, 3, 1, 2))
        outs[0].append(tok_major(with_meta(kt32, kt32_m)))
        outs[1].append(tok_major(with_meta(vt32, vt32_m)))
        outs[2].append(jnp.transpose(with_meta(lft, lft_m), (0, 2, 1)))

        qb_s, _, _, kt32_s, vt32_s, lft_s = proj(pad_rows(xs), 1, BLK)
        k_s = kt32_s[0, :, :n_dec].T.reshape(n_dec, n_heads, HEAD_DIM)
        v_s = vt32_s[0, :, :n_dec].T.reshape(n_dec, n_heads, HEAD_DIM)
        lf_s = lft_s[0, :, :n_dec].T
        qcol = qb_s[:n_dec].astype(F32).reshape(n_dec, n_heads, HEAD_DIM, 1)
        o_s = _decode_call(page_table, qcol, k_s[..., None], v_s[..., None], lf_s[..., None],
                           cache_kt, cache_vt, cache_lft, l, h_sb)
        o_s = o_s.reshape(n_dec, d)
        xs = _mlp_call(o_s[:, :w_sb], o_s[:, w_sb:], xs, n_dec, gcat, wo, gm, wu, wd)
        outs[3].append(k_s.reshape(n_dec, 1, n_heads, HEAD_DIM))
        outs[4].append(v_s.reshape(n_dec, 1, n_heads, HEAD_DIM))
        outs[5].append(lf_s.reshape(n_dec, 1, h_fox))

    y_prompt = xp.reshape(n_batch, seq, d)
    y_sample = xs.reshape(n_dec, 1, d)
    return (y_prompt, y_sample) + tuple(jnp.stack(o) for o in outs)
```

```python
import functools
import math

import jax
import jax.numpy as jnp
from jax import lax
from jax.experimental import pallas as pl
from jax.experimental.pallas import tpu as pltpu

F32 = jnp.float32
BF16 = jnp.bfloat16

HEAD_DIM = 64
N_META = 16
EPS = 1e-6
NEG = -1e30
LANES = 128
BLK = 128
VMEM_LIMIT = 56 * 1024 * 1024


def _params(sem):
    return pltpu.CompilerParams(dimension_semantics=sem, vmem_limit_bytes=VMEM_LIMIT)


def _dot(a, b):
    return jnp.dot(a, b, preferred_element_type=F32)


def _dot_nt(a, b):
    return lax.dot_general(a, b, (((1,), (1,)), ((), ())), preferred_element_type=F32)


def _dot_exact(a, b):
    return jnp.dot(a, b, preferred_element_type=F32, precision=lax.Precision.HIGHEST)


LOG2E = math.log2(math.e)
Q_SCALE = LOG2E / math.sqrt(HEAD_DIM)


def _softplus2(u):
    return jnp.maximum(u, 0.0) + jnp.log2(1.0 + jnp.exp2(-jnp.abs(u)))


def _const_spec(shape):
    nd = len(shape)
    return pl.BlockSpec(shape, lambda *_: (0,) * nd, pipeline_mode=pl.Buffered(1))


def _suffix_ones(n, dtype):
    r = lax.broadcasted_iota(jnp.int32, (n, 2 * n), 0)
    c = lax.broadcasted_iota(jnp.int32, (n, 2 * n), 1)
    return jnp.where((r > c) | (c >= n), 1.0, 0.0).astype(dtype)


def _proj_kernel(x_ref, ga_ref, wq_ref, wkvf_ref, bf_ref, gq_ref, gk_ref, grp_ref,
                 qb_ref, ktb_ref, vtb_ref, kt32_ref, vt32_ref, lft_ref):
    x = x_ref[...]
    d = x.shape[1]
    w_fox = d // 2
    n_fox = w_fox // HEAD_DIM
    tm = x.shape[0]
    inv = lax.rsqrt(jnp.mean(x * x, axis=-1, keepdims=True) + EPS)
    h = (x * inv * ga_ref[...]).astype(BF16)
    scale = Q_SCALE

    q = _dot(h, wq_ref[...])
    q_fx = q[:, w_fox:]
    sq = q_fx * q_fx
    hi = sq.astype(BF16)
    lo = (sq - hi.astype(F32)).astype(BF16)
    grp = grp_ref[...]
    ms = (_dot(hi, grp) + _dot(lo, grp)) * (1.0 / HEAD_DIM)
    q_fx = q_fx * lax.rsqrt(ms + EPS) * gq_ref[...] * scale
    qb_ref[...] = jnp.concatenate([q[:, :w_fox] * scale, q_fx], axis=1).astype(BF16)

    kvf = _dot_nt(wkvf_ref[...], h)
    k_fx = kvf[w_fox:d].reshape(n_fox, HEAD_DIM, tm)
    k_fx = k_fx * lax.rsqrt(jnp.mean(k_fx * k_fx, axis=1, keepdims=True) + EPS)
    k_fx = k_fx.reshape(w_fox, tm) * gk_ref[...]
    kt = jnp.concatenate([kvf[:w_fox], k_fx], axis=0)
    vt = kvf[d:2 * d]
    ktb_ref[0] = kt.astype(BF16)
    vtb_ref[0] = vt.astype(BF16)
    kt32_ref[0] = kt
    vt32_ref[0] = vt

    y = kvf[2 * d:2 * d + n_fox] + bf_ref[...]
    lft_ref[0] = jnp.minimum(y, 0.0) - jnp.log(1.0 + jnp.exp(-jnp.abs(y)))


def _proj_call(x2d, n_batch, tm, ga, wq, wkvf, bf, gq, gk, grp):
    rows, d = x2d.shape
    seq = rows // n_batch
    per = seq // tm
    n_f = bf.shape[0]
    row_spec = pl.BlockSpec((tm, d), lambda i: (i, 0))
    t_spec = lambda r: pl.BlockSpec((1, r, tm), lambda i: (i // per, 0, i % per))
    t_shape = lambda r, dt: jax.ShapeDtypeStruct((n_batch, r, seq), dt)
    return pl.pallas_call(
        _proj_kernel,
        grid=(rows // tm,),
        in_specs=[row_spec] + [_const_spec(a.shape) for a in (ga, wq, wkvf, bf, gq, gk, grp)],
        out_specs=[row_spec, t_spec(d), t_spec(d), t_spec(d), t_spec(d), t_spec(n_f)],
        out_shape=[jax.ShapeDtypeStruct((rows, d), BF16), t_shape(d, BF16), t_shape(d, BF16),
                   t_shape(d, F32), t_shape(d, F32), t_shape(n_f, F32)],
        compiler_params=_params(("arbitrary",)),
        name="proj",
    )(x2d, ga, wq, wkvf, bf, gq, gk, grp)


def _cumsum_kernel(lf_ref, lfmeta_ref, c_ref, cmeta_ref):
    n_batch, n_h, seq = lf_ref.shape
    r = lax.broadcasted_iota(jnp.int32, (LANES, LANES), 0)
    c = lax.broadcasted_iota(jnp.int32, (LANES, LANES), 1)
    tri = (r <= c).astype(F32)
    cmeta = _dot_exact(lfmeta_ref[...], tri)
    cmeta_ref[...] = cmeta
    base = jnp.broadcast_to(cmeta[:, LANES - 1:LANES], (n_h, LANES))
    for b in range(n_batch):
        def body(i, carry):
            off = pl.multiple_of(i * LANES, LANES)
            pre = _dot_exact(lf_ref[b, :, pl.ds(off, LANES)], tri) + carry
            c_ref[b, :, pl.ds(off, LANES)] = pre
            return jnp.broadcast_to(pre[:, LANES - 1:LANES], (n_h, LANES))
        lax.fori_loop(0, seq // LANES, body, base)


def _cumsum_call(lft_main, lft_meta):
    return pl.pallas_call(
        _cumsum_kernel,
        out_shape=[jax.ShapeDtypeStruct(lft_main.shape, F32), jax.ShapeDtypeStruct(lft_meta.shape, F32)],
        compiler_params=pltpu.CompilerParams(vmem_limit_bytes=VMEM_LIMIT),
        name="cumsum",
    )(lft_main, lft_meta)


TK = 256
TILES = 2
TQ = TILES * TK


def _diag_mask(tile, inclusive):
    row = lax.broadcasted_iota(jnp.int32, (TQ, TK), 0)
    col = lax.broadcasted_iota(jnp.int32, (TQ, TK), 1) + tile * TK
    return (col <= row) if inclusive else (col < row)


def _split_heads(q_ref):
    q = q_ref[0].astype(F32)
    low = lax.broadcasted_iota(jnp.int32, q.shape, 1) < HEAD_DIM
    return jnp.where(low, q, 0.0).astype(BF16), jnp.where(low, 0.0, q).astype(BF16)


def _sb_stages(q_ref, kt_ref, vt_ref, d_sc, tot_sc, carry_sc, acc_sc):
    suffix_ones = _suffix_ones(BLK, BF16)
    q_heads = _split_heads(q_ref)
    carry_sc[...] = jnp.zeros_like(carry_sc)
    acc_sc[...] = jnp.zeros_like(acc_sc)

    def produce(slot, kts, masks):
        work = [(h, t) for t in range(len(kts)) for h in (0, 1)]
        zs = [_dot(q_heads[h], kts[t]) for h, t in work]
        sps = []
        for z, (_, t) in zip(zs, work):
            sp = _softplus2(z)
            sps.append(sp if masks[t] is None else jnp.where(masks[t], sp, 0.0))
        spbs = [sp.astype(BF16) for sp in sps]
        rs = [[_dot(spb[:, c * BLK:(c + 1) * BLK], suffix_ones) for c in range(spb.shape[1] // BLK)]
              for spb in spbs]
        inside = [None, None]
        for k, (h, t) in enumerate(work):
            parts = [None] * len(rs[k])
            for c in reversed(range(len(rs[k]))):
                suf = rs[k][c][:, :BLK]
                parts[c] = suf if inside[h] is None else suf + inside[h]
                tot = rs[k][c][:, BLK:]
                inside[h] = tot if inside[h] is None else inside[h] + tot
            d = (zs[k] - sps[k]) - jnp.concatenate(parts, axis=1)
            if masks[t] is not None:
                d = jnp.where(masks[t], d, NEG)
            d_sc[slot, h * TILES + t, :, :d.shape[1]] = d
        for h in (0, 1):
            tot_sc[slot, h] = inside[h]

    def consume(slot, vts):
        width = vts[0].shape[1]
        probs = {}
        for h in (0, 1):
            shift = jnp.concatenate([carry_sc[h]] * (width // BLK), axis=1)
            for t in range(len(vts)):
                probs[h, t] = jnp.exp2(d_sc[slot, h * TILES + t, :, :width] - shift).astype(BF16)
        pvs = {key: _dot_nt(p, vts[key[1]]) for key, p in probs.items()}
        for h in (0, 1):
            acc_sc[h] = acc_sc[h] + sum(pvs[h, t] for t in range(len(vts)))
            carry_sc[h] = carry_sc[h] + tot_sc[slot, h]

    def tiles(ref, g):
        return [ref[0, :, pl.ds(pl.multiple_of(g * TQ + t * TK, TK), TK)] for t in reversed(range(TILES))]

    diag_masks = [_diag_mask(t, False) for t in reversed(range(TILES))]
    return produce, consume, (lambda g: tiles(kt_ref, g)), (lambda g: tiles(vt_ref, g)), diag_masks


FILL_COLS = 512


def _fox_stages(i, q_ref, kt_ref, vt_ref, ck_ref, va_sc, vb_sc, s_sc, rm_sc, m_sc, acc_sc):
    def augment(vt):
        vf = vt.astype(F32)
        top = lax.broadcasted_iota(jnp.int32, vf.shape, 0) < HEAD_DIM
        return jnp.where(top, vf, 1.0).astype(BF16), jnp.where(top, 1.0, vf).astype(BF16)

    @pl.when(i == 0)
    def _fill():
        def fill(c, carry):
            off = pl.multiple_of(c * FILL_COLS, FILL_COLS)
            va, vb = augment(vt_ref[0, :, pl.ds(off, FILL_COLS)])
            va_sc[:, pl.ds(off, FILL_COLS)] = va
            vb_sc[:, pl.ds(off, FILL_COLS)] = vb
            return carry
        lax.fori_loop(0, va_sc.shape[1] // FILL_COLS, fill, 0)

    q_heads = _split_heads(q_ref)
    m_sc[...] = jnp.full(m_sc.shape, NEG, F32)
    acc_sc[...] = jnp.zeros_like(acc_sc)

    def produce(slot, kts, cks, masks):
        qk = {(h, t): _dot(q_heads[h], kt) for t, kt in enumerate(kts) for h in (0, 1)}
        for h in (0, 1):
            folded = None
            for t, (ck2, mask) in enumerate(zip(cks, masks)):
                s = qk[h, t] - ck2[h:h + 1, :] * LOG2E
                if mask is not None:
                    s = jnp.where(mask, s, NEG)
                s_sc[slot, h * TILES + t, :, :s.shape[1]] = s
                for c in range(s.shape[1] // BLK):
                    blk = s[:, c * BLK:(c + 1) * BLK]
                    folded = blk if folded is None else jnp.maximum(folded, blk)
            rm_sc[slot, h] = jnp.max(folded, axis=-1, keepdims=True)

    def consume(slot, vs):
        width = vs[0][0].shape[1]
        for h in (0, 1):
            m = m_sc[h]
            m_new = jnp.maximum(m, rm_sc[slot, h])
            pv = sum(_dot_nt(jnp.exp2(s_sc[slot, h * TILES + t, :, :width] - m_new).astype(BF16), vs[t][h])
                     for t in range(len(vs)))
            acc_sc[h] = jnp.exp2(m - m_new) * acc_sc[h] + pv
            m_sc[h] = m_new

    def tiles(ref, g, rows):
        return [ref[rows + (pl.ds(pl.multiple_of(g * TQ + t * TK, TK), TK),)] for t in range(TILES)]

    def keys(g):
        return tiles(kt_ref, g, (0, slice(None))), tiles(ck_ref, g, (0, 0, slice(None)))

    def values(g):
        return list(zip(tiles(va_sc, g, (slice(None),)), tiles(vb_sc, g, (slice(None),))))

    diag_masks = [_diag_mask(t, True) for t in range(TILES)]
    return produce, consume, keys, values, diag_masks, augment


def _attn_kernel(qs_ref, kts_ref, vts_ref, ktms_ref, vtms_ref,
                 qf_ref, ktf_ref, vtf_ref, ktmf_ref, vtmf_ref, ck_ref, ckm_ref,
                 os_ref, of_ref,
                 d_sc, tot_sc, carry_sc, accs_sc, va_sc, vb_sc, s_sc, rm_sc, m_sc, accf_sc):
    i = pl.program_id(2)
    meta_mask = lax.broadcasted_iota(jnp.int32, (TQ, BLK), 1) < N_META
    sb_produce, sb_consume, sb_keys, sb_values, sb_diag = _sb_stages(
        qs_ref, kts_ref, vts_ref, d_sc, tot_sc, carry_sc, accs_sc)
    fx_produce, fx_consume, fx_keys, fx_values, fx_diag, augment = _fox_stages(
        i, qf_ref, ktf_ref, vtf_ref, ck_ref, va_sc, vb_sc, s_sc, rm_sc, m_sc, accf_sc)

    fx_produce(0, [ktmf_ref[...]], [ckm_ref[0]], [meta_mask])
    fx_consume(0, [augment(vtmf_ref[...])])
    sb_produce(0, sb_keys(i), sb_diag)
    fx_produce(0, *fx_keys(i), fx_diag)

    def body(j, carry):
        slot = j % 2
        sb_consume(slot, sb_values(i - j))
        fx_consume(slot, fx_values(jnp.where(j == 0, i, j - 1)))
        sb_produce(1 - slot, sb_keys(i - 1 - j), [None] * TILES)
        fx_produce(1 - slot, *fx_keys(j), [None] * TILES)
        return carry

    lax.fori_loop(0, i, body, 0)
    sb_consume(i % 2, sb_values(0))
    fx_consume(i % 2, fx_values(jnp.maximum(i - 1, 0)))
    sb_produce(0, [ktms_ref[...]], [meta_mask])
    sb_consume(0, [vtms_ref[...]])

    low = lax.broadcasted_iota(jnp.int32, (TQ, LANES), 1) < HEAD_DIM
    os_ref[0] = jnp.where(low, accs_sc[0], accs_sc[1])
    a0, a1 = accf_sc[0], accf_sc[1]
    of_ref[0] = jnp.where(low, a0 / pltpu.roll(a0, HEAD_DIM, 1), a1 / pltpu.roll(a1, HEAD_DIM, 1))


def _attn_call(qb, ktb, vtb, ktm, vtm, ck, ckm, n_pairs):
    n_batch, seq, d = qb.shape

    def specs(off):
        return [pl.BlockSpec((1, TQ, LANES), lambda b, p, i: (b, i, p + off)),
                pl.BlockSpec((1, LANES, seq), lambda b, p, i: (b, p + off, 0)),
                pl.BlockSpec((1, LANES, seq), lambda b, p, i: (b, p + off, 0)),
                pl.BlockSpec((LANES, BLK), lambda b, p, i: (p + off, 0)),
                pl.BlockSpec((LANES, BLK), lambda b, p, i: (p + off, 0))]

    out_spec = pl.BlockSpec((1, TQ, LANES), lambda b, p, i: (b, i, p))
    out_shape = jax.ShapeDtypeStruct((n_batch, seq, n_pairs * LANES), F32)
    state = lambda lanes: pltpu.VMEM((2, TQ, lanes), F32)
    return pl.pallas_call(
        _attn_kernel,
        grid=(n_batch, n_pairs, seq // TQ),
        in_specs=specs(0) + specs(n_pairs)
                 + [pl.BlockSpec((1, 1, 2, seq), lambda b, p, i: (b, p, 0, 0)),
                    pl.BlockSpec((1, 2, LANES), lambda b, p, i: (p, 0, 0))],
        out_specs=[out_spec, out_spec],
        out_shape=[out_shape, out_shape],
        scratch_shapes=[pltpu.VMEM((2, 2 * TILES, TQ, TK), F32), pltpu.VMEM((2, 2, TQ, LANES), F32),
                        state(LANES), state(LANES),
                        pltpu.VMEM((LANES, seq), BF16), pltpu.VMEM((LANES, seq), BF16),
                        pltpu.VMEM((2, 2 * TILES, TQ, TK), F32), pltpu.VMEM((2, 2, TQ, 1), F32),
                        state(1), state(LANES)],
        compiler_params=_params(("arbitrary", "arbitrary", "arbitrary")),
        name="prompt_attn",
    )(qb, ktb, vtb, ktm, vtm, qb, ktb, vtb, ktm, vtm, ck, ckm)


PAGES_PER_STEP = 16


def _decode_kernel(pt_ref, q_ref, kn_ref, vn_ref, lfn_ref, *refs, n_pages, h_sb):
    pp = PAGES_PER_STEP
    k_refs = refs[:pp]
    v_refs = refs[pp:2 * pp]
    lf_refs = refs[2 * pp:3 * pp]
    o_ref = refs[3 * pp]
    qb_sc, z_sc, lf_sc, w_sc, t_sc, acc_sc = refs[3 * pp + 1:]
    ph = pl.program_id(1)
    j = pl.program_id(2)
    n_steps = n_pages // pp
    n_heads, hd, page = qb_sc.shape

    @pl.when((ph == 0) & (j == 0))
    def _spread_q():
        for h in range(n_heads):
            qb_sc[h] = jnp.broadcast_to(q_ref[0, h], (hd, page))

    @pl.when(ph == 0)
    def _scores():
        for s in range(pp):
            r = j * pp + s
            for h in range(n_heads):
                z_sc[r, pl.ds(h, 1), :] = jnp.sum(k_refs[s][0, 0, h] * qb_sc[h], axis=0, keepdims=True)
            lf_sc[r] = lf_refs[s][0, 0]

    @pl.when((ph == 0) & (j == n_steps - 1))
    def _weights():
        z = z_sc[...]
        z_sb = z[:, :h_sb, :]
        z_fx = z[:, h_sb:, :]
        sp = _softplus2(z_sb)
        x = jnp.concatenate([sp, lf_sc[...] * LOG2E], axis=1).reshape(n_pages * n_heads, page)
        r = _dot_exact(x, _suffix_ones(page, F32))
        t_sc[...] = r[:, page:].reshape(n_pages, n_heads, page)

        def later_pages(n, carry):
            p = n_pages - 1 - n
            w_sc[p] = carry
            return carry + t_sc[p]

        lax.fori_loop(0, n_pages, later_pages, jnp.zeros((n_heads, page), F32))
        d = r[:, :page].reshape(n_pages, n_heads, page) + w_sc[...]
        a = jnp.exp2((z_sb - sp) - d[:, :h_sb, :])
        logit = z_fx + d[:, h_sb:, :] + lfn_ref[0] * LOG2E
        zs = jnp.sum(q_ref[0] * kn_ref[0], axis=1)[h_sb:]
        m = jnp.maximum(jnp.max(jnp.max(logit, axis=0), axis=-1, keepdims=True), zs)
        pw = jnp.exp2(logit - m)
        ps = jnp.exp2(zs - m)
        den = jnp.sum(jnp.sum(pw, axis=0), axis=-1, keepdims=True) + ps
        w_sc[...] = jnp.concatenate([a, pw / den], axis=1)
        pself = ps / den
        first = lax.broadcasted_iota(jnp.int32, (hd, page), 1) == 0
        for h in range(n_heads):
            if h < h_sb:
                acc_sc[h] = jnp.zeros((hd, page), F32)
            else:
                acc_sc[h] = jnp.where(first, pself[h - h_sb:h - h_sb + 1, :] * vn_ref[0, h], 0.0)

    @pl.when(ph == 1)
    def _values():
        ws = [w_sc[j * pp + s] for s in range(pp)]
        for h in range(n_heads):
            tmp = None
            for s in range(pp):
                t = ws[s][h:h + 1, :] * v_refs[s][0, 0, h]
                tmp = t if tmp is None else tmp + t
            acc_sc[h] = acc_sc[h] + tmp

    @pl.when((ph == 1) & (j == n_steps - 1))
    def _finish():
        for h in range(n_heads):
            o_ref[0, h] = jnp.sum(acc_sc[h], axis=-1, keepdims=True)


def _decode_call(page_table, qcol, kcol, vcol, lfn, cache_kt, cache_vt, cache_lft, layer, h_sb):
    n_b, n_pages = page_table.shape
    _, _, n_heads, hd, page = cache_kt.shape
    h_fox = n_heads - h_sb
    pp = PAGES_PER_STEP
    n_steps = n_pages // pp

    def k_map(s):
        return lambda b, ph, j, pt: (layer, pt[b, jnp.where(ph == 0, j, n_steps - 1) * pp + s], 0, 0, 0)

    def v_map(s):
        return lambda b, ph, j, pt: (layer, pt[b, jnp.where(ph == 1, j, 0) * pp + s], 0, 0, 0)

    def lf_map(s):
        return lambda b, ph, j, pt: (layer, pt[b, jnp.where(ph == 0, j, n_steps - 1) * pp + s], 0, 0)

    col_spec = pl.BlockSpec((1, n_heads, hd, 1), lambda b, ph, j, pt: (b, 0, 0, 0))
    page_block = (1, 1, n_heads, hd, page)
    grid_spec = pltpu.PrefetchScalarGridSpec(
        num_scalar_prefetch=1,
        grid=(n_b, 2, n_steps),
        in_specs=[col_spec, col_spec, col_spec, pl.BlockSpec((1, h_fox, 1), lambda b, ph, j, pt: (b, 0, 0))]
                 + [pl.BlockSpec(page_block, k_map(s)) for s in range(pp)]
                 + [pl.BlockSpec(page_block, v_map(s)) for s in range(pp)]
                 + [pl.BlockSpec((1, 1, h_fox, page), lf_map(s)) for s in range(pp)],
        out_specs=col_spec,
        scratch_shapes=[pltpu.VMEM((n_heads, hd, page), F32),
                        pltpu.VMEM((n_pages, n_heads, page), F32),
                        pltpu.VMEM((n_pages, h_fox, page), F32),
                        pltpu.VMEM((n_pages, n_heads, page), F32),
                        pltpu.VMEM((n_pages, n_heads, page), F32),
                        pltpu.VMEM((n_heads, hd, page), F32)],
    )
    return pl.pallas_call(
        functools.partial(_decode_kernel, n_pages=n_pages, h_sb=h_sb),
        grid_spec=grid_spec,
        out_shape=jax.ShapeDtypeStruct((n_b, n_heads, hd, 1), F32),
        compiler_params=_params(("arbitrary", "arbitrary", "arbitrary")),
        name="decode_attn",
    )(page_table, qcol, kcol, vcol, lfn, *([cache_kt] * pp), *([cache_vt] * pp), *([cache_lft] * pp))


FF_CHUNK = 1024


def _mlp_kernel(osb_ref, ofx_ref, x_ref, gcat_ref, wout_ref, gm_ref, wup_ref, wdn_ref, y_ref):
    def norm(t):
        return t * lax.rsqrt(jnp.mean(t * t, axis=-1, keepdims=True) + EPS)

    on = jnp.concatenate([norm(osb_ref[...]), norm(ofx_ref[...])], axis=1) * gcat_ref[...]
    x1 = x_ref[...] + _dot(on.astype(BF16), wout_ref[...])
    h = (norm(x1) * gm_ref[...]).astype(BF16)
    acc = jnp.zeros_like(x1)
    for c in range(wup_ref.shape[1] // FF_CHUNK):
        u = jnp.maximum(_dot(h, wup_ref[:, c * FF_CHUNK:(c + 1) * FF_CHUNK]), 0.0)
        acc = acc + _dot((u * u).astype(BF16), wdn_ref[c * FF_CHUNK:(c + 1) * FF_CHUNK, :])
    y_ref[...] = x1 + acc


def _mlp_call(o_sb, o_fx, x2d, tm, gcat, wout, gm, wup, wdn):
    rows, d = x2d.shape
    row_spec = pl.BlockSpec((tm, d), lambda i: (i, 0))
    half_spec = pl.BlockSpec((tm, d // 2), lambda i: (i, 0))
    return pl.pallas_call(
        _mlp_kernel,
        grid=(rows // tm,),
        in_specs=[half_spec, half_spec, row_spec, _const_spec(gcat.shape), _const_spec(wout.shape),
                  _const_spec(gm.shape), _const_spec(wup.shape), _const_spec(wdn.shape)],
        out_specs=row_spec,
        out_shape=jax.ShapeDtypeStruct((rows, d), F32),
        compiler_params=_params(("arbitrary",)),
        name="merge_mlp",
    )(o_sb, o_fx, x2d, gcat, wout, gm, wup, wdn)


def kernel(x_prompt, x_sample, cache_k, cache_v, cache_logf, page_table, meta_tokens, norm_attn, w_in, b_forget, q_norm, k_norm, out_norm_sb, out_norm_fox, w_out, norm_mlp, w_up, w_down):
    n_batch, seq, d = x_prompt.shape
    n_dec = x_sample.shape[0]
    depth = w_in.shape[0]
    w_sb = d // 2
    h_fox = b_forget.shape[1]
    h_sb = w_sb // HEAD_DIM
    n_heads = h_sb + h_fox
    n_pairs = w_sb // LANES

    r = lax.broadcasted_iota(jnp.int32, (w_sb, w_sb), 0) // HEAD_DIM
    c = lax.broadcasted_iota(jnp.int32, (w_sb, w_sb), 1) // HEAD_DIM
    grp = (r == c).astype(BF16)

    cache_kt = jnp.transpose(cache_k, (0, 1, 3, 4, 2))
    cache_vt = jnp.transpose(cache_v, (0, 1, 3, 4, 2))
    cache_lft = jnp.transpose(cache_logf, (0, 1, 3, 2))

    xp = x_prompt.reshape(n_batch * seq, d)
    xs = x_sample.reshape(n_dec, d)
    pad_rows = lambda a: jnp.pad(a, ((0, BLK - a.shape[0]), (0, 0)))
    outs = [[] for _ in range(6)]
    for l in range(depth):
        wt = w_in[l].T
        rows = lambda a, b: wt[a:b]
        wq = jnp.concatenate([rows(0, w_sb), rows(3 * w_sb, 4 * w_sb)], axis=0).T.astype(BF16)
        wkvf = jnp.concatenate([rows(w_sb, 2 * w_sb), rows(4 * w_sb, 5 * w_sb),
                                rows(2 * w_sb, 3 * w_sb), rows(5 * w_sb, 6 * w_sb),
                                rows(6 * w_sb, 6 * w_sb + h_fox),
                                jnp.zeros((16 - h_fox, d), w_in.dtype)], axis=0).astype(BF16)
        bf = b_forget[l].reshape(h_fox, 1)
        ga = norm_attn[l].reshape(1, d)
        gq = jnp.tile(q_norm[l], h_fox).reshape(1, w_sb)
        gk = jnp.tile(k_norm[l], h_fox).reshape(w_sb, 1)
        proj = lambda x2d, nb, tm: _proj_call(x2d, nb, tm, ga, wq, wkvf, bf, gq, gk, grp)
        gcat = jnp.concatenate([out_norm_sb[l], out_norm_fox[l]]).reshape(1, d)
        gm = norm_mlp[l].reshape(1, d)
        wo = w_out[l].astype(BF16)
        wu = w_up[l].astype(BF16)
        wd = w_down[l].astype(BF16)

        _, ktm, vtm, kt32_m, vt32_m, lft_m = proj(pad_rows(meta_tokens.astype(x_prompt.dtype)), 1, BLK)
        qb, ktb, vtb, kt32, vt32, lft = proj(xp, n_batch, 512)
        lft_meta = jnp.where(jnp.arange(BLK)[None, :] < N_META, lft_m[0], 0.0)
        c_main, c_meta = _cumsum_call(lft, lft_meta)
        qb3 = qb.reshape(n_batch, seq, d)
        o_sb, o_fx = _attn_call(qb3, ktb, vtb, ktm[0], vtm[0], c_main.reshape(n_batch, n_pairs, 2, seq),
                                c_meta.reshape(n_pairs, 2, LANES), n_pairs)
        xp = _mlp_call(o_sb.reshape(n_batch * seq, w_sb), o_fx.reshape(n_batch * seq, w_sb), xp, 256,
                       gcat, wo, gm, wu, wd)

        def with_meta(main, meta):
            m = jnp.broadcast_to(meta[:, :, :N_META], (n_batch,) + meta.shape[1:2] + (N_META,))
            return jnp.concatenate([m, main], axis=-1)

        tok_major = lambda a: jnp.transpose(a.reshape(n_batch, n_heads, HEAD_DIM, N_META + seq), (0, 3, 1, 2))
        outs[0].append(tok_major(with_meta(kt32, kt32_m)))
        outs[1].append(tok_major(with_meta(vt32, vt32_m)))
        outs[2].append(jnp.transpose(with_meta(lft, lft_m), (0, 2, 1)))

        qb_s, _, _, kt32_s, vt32_s, lft_s = proj(pad_rows(xs), 1, BLK)
        k_s = kt32_s[0, :, :n_dec].T.reshape(n_dec, n_heads, HEAD_DIM)
        v_s = vt32_s[0, :, :n_dec].T.reshape(n_dec, n_heads, HEAD_DIM)
        lf_s = lft_s[0, :, :n_dec].T
        qcol = qb_s[:n_dec].astype(F32).reshape(n_dec, n_heads, HEAD_DIM, 1)
        o_s = _decode_call(page_table, qcol, k_s[..., None], v_s[..., None], lf_s[..., None],
                           cache_kt, cache_vt, cache_lft, l, h_sb)
        o_s = o_s.reshape(n_dec, d)
        xs = _mlp_call(o_s[:, :w_sb], o_s[:, w_sb:], xs, n_dec, gcat, wo, gm, wu, wd)
        outs[3].append(k_s.reshape(n_dec, 1, n_heads, HEAD_DIM))
        outs[4].append(v_s.reshape(n_dec, 1, n_heads, HEAD_DIM))
        outs[5].append(lf_s.reshape(n_dec, 1, h_fox))

    y_prompt = xp.reshape(n_batch, seq, d)
    y_sample = xs.reshape(n_dec, 1, d)
    return (y_prompt, y_sample) + tuple(jnp.stack(o) for o in outs)
```

```python
import functools
import math

import jax
import jax.numpy as jnp
from jax import lax
from jax.experimental import pallas as pl
from jax.experimental.pallas import tpu as pltpu

F32 = jnp.float32
BF16 = jnp.bfloat16

HEAD_DIM = 64
N_META = 16
EPS = 1e-6
NEG = -1e30
LANES = 128
BLK = 128
VMEM_LIMIT = 56 * 1024 * 1024


def _params(sem):
    return pltpu.CompilerParams(dimension_semantics=sem, vmem_limit_bytes=VMEM_LIMIT)


def _dot(a, b):
    return jnp.dot(a, b, preferred_element_type=F32)


def _dot_nt(a, b):
    return lax.dot_general(a, b, (((1,), (1,)), ((), ())), preferred_element_type=F32)


def _dot_exact(a, b):
    return jnp.dot(a, b, preferred_element_type=F32, precision=lax.Precision.HIGHEST)


LOG2E = math.log2(math.e)
Q_SCALE = LOG2E / math.sqrt(HEAD_DIM)


def _softplus2(u):
    return jnp.maximum(u, 0.0) + jnp.log2(1.0 + jnp.exp2(-jnp.abs(u)))


def _const_spec(shape):
    nd = len(shape)
    return pl.BlockSpec(shape, lambda *_: (0,) * nd, pipeline_mode=pl.Buffered(1))


def _suffix_ones(n, dtype):
    r = lax.broadcasted_iota(jnp.int32, (n, 2 * n), 0)
    c = lax.broadcasted_iota(jnp.int32, (n, 2 * n), 1)
    return jnp.where((r > c) | (c >= n), 1.0, 0.0).astype(dtype)


def _proj_kernel(x_ref, ga_ref, wq_ref, wkvf_ref, bf_ref, gq_ref, gk_ref, grp_ref,
                 qb_ref, ktb_ref, vtb_ref, kt32_ref, vt32_ref, lft_ref):
    x = x_ref[...]
    d = x.shape[1]
    w_fox = d // 2
    n_fox = w_fox // HEAD_DIM
    tm = x.shape[0]
    inv = lax.rsqrt(jnp.mean(x * x, axis=-1, keepdims=True) + EPS)
    h = (x * inv * ga_ref[...]).astype(BF16)
    scale = Q_SCALE

    q = _dot(h, wq_ref[...])
    q_fx = q[:, w_fox:]
    sq = q_fx * q_fx
    hi = sq.astype(BF16)
    lo = (sq - hi.astype(F32)).astype(BF16)
    grp = grp_ref[...]
    ms = (_dot(hi, grp) + _dot(lo, grp)) * (1.0 / HEAD_DIM)
    q_fx = q_fx * lax.rsqrt(ms + EPS) * gq_ref[...] * scale
    qb_ref[...] = jnp.concatenate([q[:, :w_fox] * scale, q_fx], axis=1).astype(BF16)

    kvf = _dot_nt(wkvf_ref[...], h)
    k_fx = kvf[w_fox:d].reshape(n_fox, HEAD_DIM, tm)
    k_fx = k_fx * lax.rsqrt(jnp.mean(k_fx * k_fx, axis=1, keepdims=True) + EPS)
    k_fx = k_fx.reshape(w_fox, tm) * gk_ref[...]
    kt = jnp.concatenate([kvf[:w_fox], k_fx], axis=0)
    vt = kvf[d:2 * d]
    ktb_ref[0] = kt.astype(BF16)
    vtb_ref[0] = vt.astype(BF16)
    kt32_ref[0] = kt
    vt32_ref[0] = vt

    y = kvf[2 * d:2 * d + n_fox] + bf_ref[...]
    lft_ref[0] = jnp.minimum(y, 0.0) - jnp.log(1.0 + jnp.exp(-jnp.abs(y)))


def _proj_call(x2d, n_batch, tm, ga, wq, wkvf, bf, gq, gk, grp):
    rows, d = x2d.shape
    seq = rows // n_batch
    per = seq // tm
    n_f = bf.shape[0]
    row_spec = pl.BlockSpec((tm, d), lambda i: (i, 0))
    t_spec = lambda r: pl.BlockSpec((1, r, tm), lambda i: (i // per, 0, i % per))
    t_shape = lambda r, dt: jax.ShapeDtypeStruct((n_batch, r, seq), dt)
    return pl.pallas_call(
        _proj_kernel,
        grid=(rows // tm,),
        in_specs=[row_spec] + [_const_spec(a.shape) for a in (ga, wq, wkvf, bf, gq, gk, grp)],
        out_specs=[row_spec, t_spec(d), t_spec(d), t_spec(d), t_spec(d), t_spec(n_f)],
        out_shape=[jax.ShapeDtypeStruct((rows, d), BF16), t_shape(d, BF16), t_shape(d, BF16),
                   t_shape(d, F32), t_shape(d, F32), t_shape(n_f, F32)],
        compiler_params=_params(("arbitrary",)),
        name="proj",
    )(x2d, ga, wq, wkvf, bf, gq, gk, grp)


def _cumsum_kernel(lf_ref, lfmeta_ref, c_ref, cmeta_ref):
    n_batch, n_h, seq = lf_ref.shape
    r = lax.broadcasted_iota(jnp.int32, (LANES, LANES), 0)
    c = lax.broadcasted_iota(jnp.int32, (LANES, LANES), 1)
    tri = (r <= c).astype(F32)
    cmeta = _dot_exact(lfmeta_ref[...], tri)
    cmeta_ref[...] = cmeta
    base = jnp.broadcast_to(cmeta[:, LANES - 1:LANES], (n_h, LANES))
    for b in range(n_batch):
        def body(i, carry):
            off = pl.multiple_of(i * LANES, LANES)
            pre = _dot_exact(lf_ref[b, :, pl.ds(off, LANES)], tri) + carry
            c_ref[b, :, pl.ds(off, LANES)] = pre
            return jnp.broadcast_to(pre[:, LANES - 1:LANES], (n_h, LANES))
        lax.fori_loop(0, seq // LANES, body, base)


def _cumsum_call(lft_main, lft_meta):
    return pl.pallas_call(
        _cumsum_kernel,
        out_shape=[jax.ShapeDtypeStruct(lft_main.shape, F32), jax.ShapeDtypeStruct(lft_meta.shape, F32)],
        compiler_params=pltpu.CompilerParams(vmem_limit_bytes=VMEM_LIMIT),
        name="cumsum",
    )(lft_main, lft_meta)


TK = 256
TILES = 2
TQ = TILES * TK


def _diag_mask(tile, inclusive):
    row = lax.broadcasted_iota(jnp.int32, (TQ, TK), 0)
    col = lax.broadcasted_iota(jnp.int32, (TQ, TK), 1) + tile * TK
    return (col <= row) if inclusive else (col < row)


def _split_heads(q_ref):
    q = q_ref[0].astype(F32)
    low = lax.broadcasted_iota(jnp.int32, q.shape, 1) < HEAD_DIM
    return jnp.where(low, q, 0.0).astype(BF16), jnp.where(low, 0.0, q).astype(BF16)


def _sb_stages(q_ref, kt_ref, vt_ref, d_sc, tot_sc, carry_sc, acc_sc):
    suffix_ones = _suffix_ones(BLK, BF16)
    q_heads = _split_heads(q_ref)
    carry_sc[...] = jnp.zeros_like(carry_sc)
    acc_sc[...] = jnp.zeros_like(acc_sc)

    def produce(slot, kts, masks):
        work = [(h, t) for t in range(len(kts)) for h in (0, 1)]
        zs = [_dot(q_heads[h], kts[t]) for h, t in work]
        sps = []
        for z, (_, t) in zip(zs, work):
            sp = _softplus2(z)
            sps.append(sp if masks[t] is None else jnp.where(masks[t], sp, 0.0))
        spbs = [sp.astype(BF16) for sp in sps]
        rs = [[_dot(spb[:, c * BLK:(c + 1) * BLK], suffix_ones) for c in range(spb.shape[1] // BLK)]
              for spb in spbs]
        inside = [None, None]
        for k, (h, t) in enumerate(work):
            parts = [None] * len(rs[k])
            for c in reversed(range(len(rs[k]))):
                suf = rs[k][c][:, :BLK]
                parts[c] = suf if inside[h] is None else suf + inside[h]
                tot = rs[k][c][:, BLK:]
                inside[h] = tot if inside[h] is None else inside[h] + tot
            d = (zs[k] - sps[k]) - jnp.concatenate(parts, axis=1)
            if masks[t] is not None:
                d = jnp.where(masks[t], d, NEG)
            d_sc[slot, h * TILES + t, :, :d.shape[1]] = d
        for h in (0, 1):
            tot_sc[slot, h] = inside[h]

    def consume(slot, vts):
        width = vts[0].shape[1]
        probs = {}
        for h in (0, 1):
            shift = jnp.concatenate([carry_sc[h]] * (width // BLK), axis=1)
            for t in range(len(vts)):
                probs[h, t] = jnp.exp2(d_sc[slot, h * TILES + t, :, :width] - shift).astype(BF16)
        pvs = {key: _dot_nt(p, vts[key[1]]) for key, p in probs.items()}
        for h in (0, 1):
            acc_sc[h] = acc_sc[h] + sum(pvs[h, t] for t in range(len(vts)))
            carry_sc[h] = carry_sc[h] + tot_sc[slot, h]

    def tiles(ref, g):
        return [ref[0, :, pl.ds(pl.multiple_of(g * TQ + t * TK, TK), TK)] for t in reversed(range(TILES))]

    diag_masks = [_diag_mask(t, False) for t in reversed(range(TILES))]
    return produce, consume, (lambda g: tiles(kt_ref, g)), (lambda g: tiles(vt_ref, g)), diag_masks


FILL_COLS = 512


def _fox_stages(i, q_ref, kt_ref, vt_ref, ck_ref, va_sc, vb_sc, s_sc, rm_sc, m_sc, acc_sc):
    def augment(vt):
        vf = vt.astype(F32)
        top = lax.broadcasted_iota(jnp.int32, vf.shape, 0) < HEAD_DIM
        return jnp.where(top, vf, 1.0).astype(BF16), jnp.where(top, 1.0, vf).astype(BF16)

    @pl.when(i == 0)
    def _fill():
        def fill(c, carry):
            off = pl.multiple_of(c * FILL_COLS, FILL_COLS)
            va, vb = augment(vt_ref[0, :, pl.ds(off, FILL_COLS)])
            va_sc[:, pl.ds(off, FILL_COLS)] = va
            vb_sc[:, pl.ds(off, FILL_COLS)] = vb
            return carry
        lax.fori_loop(0, va_sc.shape[1] // FILL_COLS, fill, 0)

    q_heads = _split_heads(q_ref)
    m_sc[...] = jnp.full(m_sc.shape, NEG, F32)
    acc_sc[...] = jnp.zeros_like(acc_sc)

    def produce(slot, kts, cks, masks):
        qk = {(h, t): _dot(q_heads[h], kt) for t, kt in enumerate(kts) for h in (0, 1)}
        for h in (0, 1):
            folded = None
            for t, (ck2, mask) in enumerate(zip(cks, masks)):
                s = qk[h, t] - ck2[h:h + 1, :] * LOG2E
                if mask is not None:
                    s = jnp.where(mask, s, NEG)
                s_sc[slot, h * TILES + t, :, :s.shape[1]] = s
                for c in range(s.shape[1] // BLK):
                    blk = s[:, c * BLK:(c + 1) * BLK]
                    folded = blk if folded is None else jnp.maximum(folded, blk)
            rm_sc[slot, h] = jnp.max(folded, axis=-1, keepdims=True)

    def consume(slot, vs):
        width = vs[0][0].shape[1]
        for h in (0, 1):
            m = m_sc[h]
            m_new = jnp.maximum(m, rm_sc[slot, h])
            pv = sum(_dot_nt(jnp.exp2(s_sc[slot, h * TILES + t, :, :width] - m_new).astype(BF16), vs[t][h])
                     for t in range(len(vs)))
            acc_sc[h] = jnp.exp2(m - m_new) * acc_sc[h] + pv
            m_sc[h] = m_new

    def tiles(ref, g, rows):
        return [ref[rows + (pl.ds(pl.multiple_of(g * TQ + t * TK, TK), TK),)] for t in range(TILES)]

    def keys(g):
        return tiles(kt_ref, g, (0, slice(None))), tiles(ck_ref, g, (0, 0, slice(None)))

    def values(g):
        return list(zip(tiles(va_sc, g, (slice(None),)), tiles(vb_sc, g, (slice(None),))))

    diag_masks = [_diag_mask(t, True) for t in range(TILES)]
    return produce, consume, keys, values, diag_masks, augment


DEAD = 160.0


def _sb_kernel(q_ref, kt_ref, vt_ref, ktm_ref, vtm_ref, o_ref, d_sc, tot_sc, carry_sc, acc_sc):
    i = pl.program_id(2)
    produce, consume, keys, values, diag_masks = _sb_stages(q_ref, kt_ref, vt_ref, d_sc, tot_sc, carry_sc, acc_sc)
    produce(0, keys(i), diag_masks)

    def cond(state):
        j, smallest_carry = state
        return (j < i) & (smallest_carry < DEAD)

    def body(state):
        j, _ = state
        slot = j % 2
        consume(slot, values(i - j))
        produce(1 - slot, keys(i - 1 - j), [None] * TILES)
        return j + 1, jnp.min(carry_sc[...])

    j_end, smallest_carry = lax.while_loop(cond, body, (jnp.int32(0), jnp.float32(0.0)))

    @pl.when(smallest_carry < DEAD)
    def _tail():
        consume(j_end % 2, values(i - j_end))
        produce(0, [ktm_ref[...]], [lax.broadcasted_iota(jnp.int32, (TQ, BLK), 1) < N_META])
        consume(0, [vtm_ref[...]])

    low = lax.broadcasted_iota(jnp.int32, (TQ, LANES), 1) < HEAD_DIM
    o_ref[0] = jnp.where(low, acc_sc[0], acc_sc[1])


def _fox_kernel(q_ref, kt_ref, vt_ref, ktm_ref, vtm_ref, ck_ref, ckm_ref, o_ref,
                va_sc, vb_sc, s_sc, rm_sc, m_sc, acc_sc, knorm_sc):
    i = pl.program_id(2)
    produce, consume, keys, values, diag_masks, augment = _fox_stages(
        i, q_ref, kt_ref, vt_ref, ck_ref, va_sc, vb_sc, s_sc, rm_sc, m_sc, acc_sc)

    def head_sumsq(kt):
        sq = kt.astype(F32)
        sq = sq * sq
        return jnp.concatenate([jnp.sum(sq[:HEAD_DIM], axis=0, keepdims=True),
                                jnp.sum(sq[HEAD_DIM:], axis=0, keepdims=True)], axis=0)

    @pl.when(i == 0)
    def _key_norms():
        def chunk(c, best):
            off = pl.multiple_of(c * FILL_COLS, FILL_COLS)
            return jnp.maximum(best, head_sumsq(kt_ref[0, :, pl.ds(off, FILL_COLS)]))
        best = lax.fori_loop(0, kt_ref.shape[2] // FILL_COLS, chunk, jnp.zeros((2, FILL_COLS), F32))
        largest = jnp.maximum(jnp.max(best, axis=-1, keepdims=True),
                              jnp.max(head_sumsq(ktm_ref[...]), axis=-1, keepdims=True))
        knorm_sc[...] = jnp.sqrt(largest)

    qf = q_ref[0].astype(F32)
    qsq = qf * qf
    low_q = lax.broadcasted_iota(jnp.int32, qsq.shape, 1) < HEAD_DIM
    q_norms = [jnp.sqrt(jnp.sum(jnp.where(low_q, qsq, 0.0), axis=-1, keepdims=True)),
               jnp.sqrt(jnp.sum(jnp.where(low_q, 0.0, qsq), axis=-1, keepdims=True))]

    def alive(g):
        last = ck_ref[0, 0, :, pl.ds(pl.multiple_of(g * TQ + TQ - BLK, BLK), BLK)][:, BLK - 1:BLK]
        gap = [q_norms[h] * knorm_sc[h:h + 1, :] - last[h:h + 1, :] * LOG2E - m_sc[h] for h in (0, 1)]
        return (jnp.max(jnp.maximum(gap[0], gap[1])) > -DEAD).astype(jnp.int32)

    produce(0, *keys(i), diag_masks)

    def cond(state):
        j, go = state
        return (j < i) & (go > 0)

    def body(state):
        j, _ = state
        slot = j % 2
        consume(slot, values(i - j))
        produce(1 - slot, *keys(i - 1 - j), [None] * TILES)
        return j + 1, alive(i - 1 - j)

    j_end, go = lax.while_loop(cond, body, (jnp.int32(0), jnp.int32(1)))

    @pl.when(go > 0)
    def _tail():
        consume(j_end % 2, values(i - j_end))
        produce(0, [ktm_ref[...]], [ckm_ref[0]], [lax.broadcasted_iota(jnp.int32, (TQ, BLK), 1) < N_META])
        consume(0, [augment(vtm_ref[...])])

    low = lax.broadcasted_iota(jnp.int32, (TQ, LANES), 1) < HEAD_DIM
    a0, a1 = acc_sc[0], acc_sc[1]
    o_ref[0] = jnp.where(low, a0 / pltpu.roll(a0, HEAD_DIM, 1), a1 / pltpu.roll(a1, HEAD_DIM, 1))


def _attn_call(qb, ktb, vtb, ktm, vtm, ck, ckm, n_pairs):
    n_batch, seq, d = qb.shape

    def specs(off):
        return [pl.BlockSpec((1, TQ, LANES), lambda b, p, i: (b, i, p + off)),
                pl.BlockSpec((1, LANES, seq), lambda b, p, i: (b, p + off, 0)),
                pl.BlockSpec((1, LANES, seq), lambda b, p, i: (b, p + off, 0)),
                pl.BlockSpec((LANES, BLK), lambda b, p, i: (p + off, 0)),
                pl.BlockSpec((LANES, BLK), lambda b, p, i: (p + off, 0))]

    common = dict(
        grid=(n_batch, n_pairs, seq // TQ),
        out_specs=pl.BlockSpec((1, TQ, LANES), lambda b, p, i: (b, i, p)),
        out_shape=jax.ShapeDtypeStruct((n_batch, seq, n_pairs * LANES), F32),
        compiler_params=_params(("arbitrary", "arbitrary", "arbitrary")))
    staged = pltpu.VMEM((2, 2 * TILES, TQ, TK), F32)
    state = lambda lanes: pltpu.VMEM((2, TQ, lanes), F32)
    o_sb = pl.pallas_call(
        _sb_kernel, in_specs=specs(0),
        scratch_shapes=[staged, pltpu.VMEM((2, 2, TQ, LANES), F32), state(LANES), state(LANES)],
        name="sb_attn", **common,
    )(qb, ktb, vtb, ktm, vtm)
    o_fx = pl.pallas_call(
        _fox_kernel,
        in_specs=specs(n_pairs) + [pl.BlockSpec((1, 1, 2, seq), lambda b, p, i: (b, p, 0, 0)),
                                   pl.BlockSpec((1, 2, LANES), lambda b, p, i: (p, 0, 0))],
        scratch_shapes=[pltpu.VMEM((LANES, seq), BF16), pltpu.VMEM((LANES, seq), BF16),
                        staged, pltpu.VMEM((2, 2, TQ, 1), F32), state(1), state(LANES),
                        pltpu.VMEM((2, 1), F32)],
        name="fox_attn", **common,
    )(qb, ktb, vtb, ktm, vtm, ck, ckm)
    return o_sb, o_fx


PAGES_PER_STEP = 16


def _decode_kernel(pt_ref, q_ref, kn_ref, vn_ref, lfn_ref, *refs, n_pages, h_sb):
    pp = PAGES_PER_STEP
    k_refs = refs[:pp]
    v_refs = refs[pp:2 * pp]
    lf_refs = refs[2 * pp:3 * pp]
    o_ref = refs[3 * pp]
    qb_sc, z_sc, lf_sc, w_sc, t_sc, acc_sc = refs[3 * pp + 1:]
    ph = pl.program_id(1)
    j = pl.program_id(2)
    n_steps = n_pages // pp
    n_heads, hd, page = qb_sc.shape

    @pl.when((ph == 0) & (j == 0))
    def _spread_q():
        for h in range(n_heads):
            qb_sc[h] = jnp.broadcast_to(q_ref[0, h], (hd, page))

    @pl.when(ph == 0)
    def _scores():
        for s in range(pp):
            r = j * pp + s
            for h in range(n_heads):
                z_sc[r, pl.ds(h, 1), :] = jnp.sum(k_refs[s][0, 0, h] * qb_sc[h], axis=0, keepdims=True)
            lf_sc[r] = lf_refs[s][0, 0]

    @pl.when((ph == 0) & (j == n_steps - 1))
    def _weights():
        z = z_sc[...]
        z_sb = z[:, :h_sb, :]
        z_fx = z[:, h_sb:, :]
        sp = _softplus2(z_sb)
        x = jnp.concatenate([sp, lf_sc[...] * LOG2E], axis=1).reshape(n_pages * n_heads, page)
        r = _dot_exact(x, _suffix_ones(page, F32))
        t_sc[...] = r[:, page:].reshape(n_pages, n_heads, page)

        def later_pages(n, carry):
            p = n_pages - 1 - n
            w_sc[p] = carry
            return carry + t_sc[p]

        lax.fori_loop(0, n_pages, later_pages, jnp.zeros((n_heads, page), F32))
        d = r[:, :page].reshape(n_pages, n_heads, page) + w_sc[...]
        a = jnp.exp2((z_sb - sp) - d[:, :h_sb, :])
        logit = z_fx + d[:, h_sb:, :] + lfn_ref[0] * LOG2E
        zs = jnp.sum(q_ref[0] * kn_ref[0], axis=1)[h_sb:]
        m = jnp.maximum(jnp.max(jnp.max(logit, axis=0), axis=-1, keepdims=True), zs)
        pw = jnp.exp2(logit - m)
        ps = jnp.exp2(zs - m)
        den = jnp.sum(jnp.sum(pw, axis=0), axis=-1, keepdims=True) + ps
        w_sc[...] = jnp.concatenate([a, pw / den], axis=1)
        pself = ps / den
        first = lax.broadcasted_iota(jnp.int32, (hd, page), 1) == 0
        for h in range(n_heads):
            if h < h_sb:
                acc_sc[h] = jnp.zeros((hd, page), F32)
            else:
                acc_sc[h] = jnp.where(first, pself[h - h_sb:h - h_sb + 1, :] * vn_ref[0, h], 0.0)

    @pl.when(ph == 1)
    def _values():
        ws = [w_sc[j * pp + s] for s in range(pp)]
        for h in range(n_heads):
            tmp = None
            for s in range(pp):
                t = ws[s][h:h + 1, :] * v_refs[s][0, 0, h]
                tmp = t if tmp is None else tmp + t
            acc_sc[h] = acc_sc[h] + tmp

    @pl.when((ph == 1) & (j == n_steps - 1))
    def _finish():
        for h in range(n_heads):
            o_ref[0, h] = jnp.sum(acc_sc[h], axis=-1, keepdims=True)


def _decode_call(page_table, qcol, kcol, vcol, lfn, cache_kt, cache_vt, cache_lft, layer, h_sb):
    n_b, n_pages = page_table.shape
    _, _, n_heads, hd, page = cache_kt.shape
    h_fox = n_heads - h_sb
    pp = PAGES_PER_STEP
    n_steps = n_pages // pp

    def k_map(s):
        return lambda b, ph, j, pt: (layer, pt[b, jnp.where(ph == 0, j, n_steps - 1) * pp + s], 0, 0, 0)

    def v_map(s):
        return lambda b, ph, j, pt: (layer, pt[b, jnp.where(ph == 1, j, 0) * pp + s], 0, 0, 0)

    def lf_map(s):
        return lambda b, ph, j, pt: (layer, pt[b, jnp.where(ph == 0, j, n_steps - 1) * pp + s], 0, 0)

    col_spec = pl.BlockSpec((1, n_heads, hd, 1), lambda b, ph, j, pt: (b, 0, 0, 0))
    page_block = (1, 1, n_heads, hd, page)
    grid_spec = pltpu.PrefetchScalarGridSpec(
        num_scalar_prefetch=1,
        grid=(n_b, 2, n_steps),
        in_specs=[col_spec, col_spec, col_spec, pl.BlockSpec((1, h_fox, 1), lambda b, ph, j, pt: (b, 0, 0))]
                 + [pl.BlockSpec(page_block, k_map(s)) for s in range(pp)]
                 + [pl.BlockSpec(page_block, v_map(s)) for s in range(pp)]
                 + [pl.BlockSpec((1, 1, h_fox, page), lf_map(s)) for s in range(pp)],
        out_specs=col_spec,
        scratch_shapes=[pltpu.VMEM((n_heads, hd, page), F32),
                        pltpu.VMEM((n_pages, n_heads, page), F32),
                        pltpu.VMEM((n_pages, h_fox, page), F32),
                        pltpu.VMEM((n_pages, n_heads, page), F32),
                        pltpu.VMEM((n_pages, n_heads, page), F32),
                        pltpu.VMEM((n_heads, hd, page), F32)],
    )
    return pl.pallas_call(
        functools.partial(_decode_kernel, n_pages=n_pages, h_sb=h_sb),
        grid_spec=grid_spec,
        out_shape=jax.ShapeDtypeStruct((n_b, n_heads, hd, 1), F32),
        compiler_params=_params(("arbitrary", "arbitrary", "arbitrary")),
        name="decode_attn",
    )(page_table, qcol, kcol, vcol, lfn, *([cache_kt] * pp), *([cache_vt] * pp), *([cache_lft] * pp))


FF_CHUNK = 1024


def _mlp_kernel(osb_ref, ofx_ref, x_ref, gcat_ref, wout_ref, gm_ref, wup_ref, wdn_ref, y_ref):
    def norm(t):
        return t * lax.rsqrt(jnp.mean(t * t, axis=-1, keepdims=True) + EPS)

    on = jnp.concatenate([norm(osb_ref[...]), norm(ofx_ref[...])], axis=1) * gcat_ref[...]
    x1 = x_ref[...] + _dot(on.astype(BF16), wout_ref[...])
    h = (norm(x1) * gm_ref[...]).astype(BF16)
    acc = jnp.zeros_like(x1)
    for c in range(wup_ref.shape[1] // FF_CHUNK):
        u = jnp.maximum(_dot(h, wup_ref[:, c * FF_CHUNK:(c + 1) * FF_CHUNK]), 0.0)
        acc = acc + _dot((u * u).astype(BF16), wdn_ref[c * FF_CHUNK:(c + 1) * FF_CHUNK, :])
    y_ref[...] = x1 + acc


def _mlp_call(o_sb, o_fx, x2d, tm, gcat, wout, gm, wup, wdn):
    rows, d = x2d.shape
    row_spec = pl.BlockSpec((tm, d), lambda i: (i, 0))
    half_spec = pl.BlockSpec((tm, d // 2), lambda i: (i, 0))
    return pl.pallas_call(
        _mlp_kernel,
        grid=(rows // tm,),
        in_specs=[half_spec, half_spec, row_spec, _const_spec(gcat.shape), _const_spec(wout.shape),
                  _const_spec(gm.shape), _const_spec(wup.shape), _const_spec(wdn.shape)],
        out_specs=row_spec,
        out_shape=jax.ShapeDtypeStruct((rows, d), F32),
        compiler_params=_params(("arbitrary",)),
        name="merge_mlp",
    )(o_sb, o_fx, x2d, gcat, wout, gm, wup, wdn)


def kernel(x_prompt, x_sample, cache_k, cache_v, cache_logf, page_table, meta_tokens, norm_attn, w_in, b_forget, q_norm, k_norm, out_norm_sb, out_norm_fox, w_out, norm_mlp, w_up, w_down):
    n_batch, seq, d = x_prompt.shape
    n_dec = x_sample.shape[0]
    depth = w_in.shape[0]
    w_sb = d // 2
    h_fox = b_forget.shape[1]
    h_sb = w_sb // HEAD_DIM
    n_heads = h_sb + h_fox
    n_pairs = w_sb // LANES

    r = lax.broadcasted_iota(jnp.int32, (w_sb, w_sb), 0) // HEAD_DIM
    c = lax.broadcasted_iota(jnp.int32, (w_sb, w_sb), 1) // HEAD_DIM
    grp = (r == c).astype(BF16)

    cache_kt = jnp.transpose(cache_k, (0, 1, 3, 4, 2))
    cache_vt = jnp.transpose(cache_v, (0, 1, 3, 4, 2))
    cache_lft = jnp.transpose(cache_logf, (0, 1, 3, 2))

    xp = x_prompt.reshape(n_batch * seq, d)
    xs = x_sample.reshape(n_dec, d)
    pad_rows = lambda a: jnp.pad(a, ((0, BLK - a.shape[0]), (0, 0)))
    outs = [[] for _ in range(6)]
    for l in range(depth):
        wt = w_in[l].T
        rows = lambda a, b: wt[a:b]
        wq = jnp.concatenate([rows(0, w_sb), rows(3 * w_sb, 4 * w_sb)], axis=0).T.astype(BF16)
        wkvf = jnp.concatenate([rows(w_sb, 2 * w_sb), rows(4 * w_sb, 5 * w_sb),
                                rows(2 * w_sb, 3 * w_sb), rows(5 * w_sb, 6 * w_sb),
                                rows(6 * w_sb, 6 * w_sb + h_fox),
                                jnp.zeros((16 - h_fox, d), w_in.dtype)], axis=0).astype(BF16)
        bf = b_forget[l].reshape(h_fox, 1)
        ga = norm_attn[l].reshape(1, d)
        gq = jnp.tile(q_norm[l], h_fox).reshape(1, w_sb)
        gk = jnp.tile(k_norm[l], h_fox).reshape(w_sb, 1)
        proj = lambda x2d, nb, tm: _proj_call(x2d, nb, tm, ga, wq, wkvf, bf, gq, gk, grp)
        gcat = jnp.concatenate([out_norm_sb[l], out_norm_fox[l]]).reshape(1, d)
        gm = norm_mlp[l].reshape(1, d)
        wo = w_out[l].astype(BF16)
        wu = w_up[l].astype(BF16)
        wd = w_down[l].astype(BF16)

        _, ktm, vtm, kt32_m, vt32_m, lft_m = proj(pad_rows(meta_tokens.astype(x_prompt.dtype)), 1, BLK)
        qb, ktb, vtb, kt32, vt32, lft = proj(xp, n_batch, 512)
        lft_meta = jnp.where(jnp.arange(BLK)[None, :] < N_META, lft_m[0], 0.0)
        c_main, c_meta = _cumsum_call(lft, lft_meta)
        qb3 = qb.reshape(n_batch, seq, d)
        o_sb, o_fx = _attn_call(qb3, ktb, vtb, ktm[0], vtm[0], c_main.reshape(n_batch, n_pairs, 2, seq),
                                c_meta.reshape(n_pairs, 2, LANES), n_pairs)
        xp = _mlp_call(o_sb.reshape(n_batch * seq, w_sb), o_fx.reshape(n_batch * seq, w_sb), xp, 512,
                       gcat, wo, gm, wu, wd)

        def with_meta(main, meta):
            m = jnp.broadcast_to(meta[:, :, :N_META], (n_batch,) + meta.shape[1:2] + (N_META,))
            return jnp.concatenate([m, main], axis=-1)

        tok_major = lambda a: jnp.transpose(a.reshape(n_batch, n_heads, HEAD_DIM, N_META + seq), (0, 3, 1, 2))
        outs[0].append(tok_major(with_meta(kt32, kt32_m)))
        outs[1].append(tok_major(with_meta(vt32, vt32_m)))
        outs[2].append(jnp.transpose(with_meta(lft, lft_m), (0, 2, 1)))

        qb_s, _, _, kt32_s, vt32_s, lft_s = proj(pad_rows(xs), 1, BLK)
        k_s = kt32_s[0, :, :n_dec].T.reshape(n_dec, n_heads, HEAD_DIM)
        v_s = vt32_s[0, :, :n_dec].T.reshape(n_dec, n_heads, HEAD_DIM)
        lf_s = lft_s[0, :, :n_dec].T
        qcol = qb_s[:n_dec].astype(F32).reshape(n_dec, n_heads, HEAD_DIM, 1)
        o_s = _decode_call(page_table, qcol, k_s[..., None], v_s[..., None], lf_s[..., None],
                           cache_kt, cache_vt, cache_lft, l, h_sb)
        o_s = o_s.reshape(n_dec, d)
        xs = _mlp_call(o_s[:, :w_sb], o_s[:, w_sb:], xs, n_dec, gcat, wo, gm, wu, wd)
        outs[3].append(k_s.reshape(n_dec, 1, n_heads, HEAD_DIM))
        outs[4].append(v_s.reshape(n_dec, 1, n_heads, HEAD_DIM))
        outs[5].append(lf_s.reshape(n_dec, 1, h_fox))

    y_prompt = xp.reshape(n_batch, seq, d)
    y_sample = xs.reshape(n_dec, 1, d)
    return (y_prompt, y_sample) + tuple(jnp.stack(o) for o in outs)
```

```python
import functools
import math

import jax
import jax.numpy as jnp
from jax import lax
from jax.experimental import pallas as pl
from jax.experimental.pallas import tpu as pltpu

F32 = jnp.float32
BF16 = jnp.bfloat16

HEAD_DIM = 64
N_META = 16
EPS = 1e-6
NEG = -1e30
LANES = 128
BLK = 128
VMEM_LIMIT = 56 * 1024 * 1024


def _params(sem):
    return pltpu.CompilerParams(dimension_semantics=sem, vmem_limit_bytes=VMEM_LIMIT)


def _dot(a, b):
    return jnp.dot(a, b, preferred_element_type=F32)


def _dot_nt(a, b):
    return lax.dot_general(a, b, (((1,), (1,)), ((), ())), preferred_element_type=F32)


def _dot_exact(a, b):
    return jnp.dot(a, b, preferred_element_type=F32, precision=lax.Precision.HIGHEST)


LOG2E = math.log2(math.e)
Q_SCALE = LOG2E / math.sqrt(HEAD_DIM)


def _softplus2(u):
    return jnp.maximum(u, 0.0) + jnp.log2(1.0 + jnp.exp2(-jnp.abs(u)))


def _const_spec(shape):
    nd = len(shape)
    return pl.BlockSpec(shape, lambda *_: (0,) * nd, pipeline_mode=pl.Buffered(1))


def _suffix_ones(n, dtype):
    r = lax.broadcasted_iota(jnp.int32, (n, 2 * n), 0)
    c = lax.broadcasted_iota(jnp.int32, (n, 2 * n), 1)
    return jnp.where((r > c) | (c >= n), 1.0, 0.0).astype(dtype)


def _proj_kernel(x_ref, ga_ref, wq_ref, wkvf_ref, bf_ref, gq_ref, gk_ref, grp_ref,
                 qb_ref, ktb_ref, vtb_ref, kt32_ref, vt32_ref, lft_ref):
    x = x_ref[...]
    d = x.shape[1]
    w_fox = d // 2
    n_fox = w_fox // HEAD_DIM
    tm = x.shape[0]
    inv = lax.rsqrt(jnp.mean(x * x, axis=-1, keepdims=True) + EPS)
    h = (x * inv * ga_ref[...]).astype(BF16)
    scale = Q_SCALE

    q = _dot(h, wq_ref[...])
    q_fx = q[:, w_fox:]
    sq = q_fx * q_fx
    hi = sq.astype(BF16)
    lo = (sq - hi.astype(F32)).astype(BF16)
    grp = grp_ref[...]
    ms = (_dot(hi, grp) + _dot(lo, grp)) * (1.0 / HEAD_DIM)
    q_fx = q_fx * lax.rsqrt(ms + EPS) * gq_ref[...] * scale
    qb_ref[...] = jnp.concatenate([q[:, :w_fox] * scale, q_fx], axis=1).astype(BF16)

    kvf = _dot_nt(wkvf_ref[...], h)
    k_fx = kvf[w_fox:d].reshape(n_fox, HEAD_DIM, tm)
    k_fx = k_fx * lax.rsqrt(jnp.mean(k_fx * k_fx, axis=1, keepdims=True) + EPS)
    k_fx = k_fx.reshape(w_fox, tm) * gk_ref[...]
    kt = jnp.concatenate([kvf[:w_fox], k_fx], axis=0)
    vt = kvf[d:2 * d]
    ktb_ref[0] = kt.astype(BF16)
    vtb_ref[0] = vt.astype(BF16)
    kt32_ref[0] = kt
    vt32_ref[0] = vt

    y = kvf[2 * d:2 * d + n_fox] + bf_ref[...]
    lft_ref[0] = jnp.minimum(y, 0.0) - jnp.log(1.0 + jnp.exp(-jnp.abs(y)))


def _proj_call(x2d, n_batch, tm, ga, wq, wkvf, bf, gq, gk, grp):
    rows, d = x2d.shape
    seq = rows // n_batch
    per = seq // tm
    n_f = bf.shape[0]
    row_spec = pl.BlockSpec((tm, d), lambda i: (i, 0))
    t_spec = lambda r: pl.BlockSpec((1, r, tm), lambda i: (i // per, 0, i % per))
    t_shape = lambda r, dt: jax.ShapeDtypeStruct((n_batch, r, seq), dt)
    return pl.pallas_call(
        _proj_kernel,
        grid=(rows // tm,),
        in_specs=[row_spec] + [_const_spec(a.shape) for a in (ga, wq, wkvf, bf, gq, gk, grp)],
        out_specs=[row_spec, t_spec(d), t_spec(d), t_spec(d), t_spec(d), t_spec(n_f)],
        out_shape=[jax.ShapeDtypeStruct((rows, d), BF16), t_shape(d, BF16), t_shape(d, BF16),
                   t_shape(d, F32), t_shape(d, F32), t_shape(n_f, F32)],
        compiler_params=_params(("arbitrary",)),
        name="proj",
    )(x2d, ga, wq, wkvf, bf, gq, gk, grp)


def _cumsum_kernel(lf_ref, lfmeta_ref, c_ref, cmeta_ref):
    n_batch, n_h, seq = lf_ref.shape
    r = lax.broadcasted_iota(jnp.int32, (LANES, LANES), 0)
    c = lax.broadcasted_iota(jnp.int32, (LANES, LANES), 1)
    tri = (r <= c).astype(F32)
    cmeta = _dot_exact(lfmeta_ref[...], tri)
    cmeta_ref[...] = cmeta
    base = jnp.broadcast_to(cmeta[:, LANES - 1:LANES], (n_h, LANES))
    for b in range(n_batch):
        def body(i, carry):
            off = pl.multiple_of(i * LANES, LANES)
            pre = _dot_exact(lf_ref[b, :, pl.ds(off, LANES)], tri) + carry
            c_ref[b, :, pl.ds(off, LANES)] = pre
            return jnp.broadcast_to(pre[:, LANES - 1:LANES], (n_h, LANES))
        lax.fori_loop(0, seq // LANES, body, base)


def _cumsum_call(lft_main, lft_meta):
    return pl.pallas_call(
        _cumsum_kernel,
        out_shape=[jax.ShapeDtypeStruct(lft_main.shape, F32), jax.ShapeDtypeStruct(lft_meta.shape, F32)],
        compiler_params=pltpu.CompilerParams(vmem_limit_bytes=VMEM_LIMIT),
        name="cumsum",
    )(lft_main, lft_meta)


class _Tiling:
    def __init__(self, tk, tiles):
        self.tk, self.tiles, self.tq = tk, tiles, tk * tiles

    def diag_mask(self, tile, inclusive):
        row = lax.broadcasted_iota(jnp.int32, (self.tq, self.tk), 0)
        col = lax.broadcasted_iota(jnp.int32, (self.tq, self.tk), 1) + tile * self.tk
        return (col <= row) if inclusive else (col < row)

    def meta_mask(self):
        return lax.broadcasted_iota(jnp.int32, (self.tq, BLK), 1) < N_META


SB_TILING = _Tiling(256, 1)
FOX_TILING = _Tiling(256, 2)


def _split_heads(q_ref):
    q = q_ref[0].astype(F32)
    low = lax.broadcasted_iota(jnp.int32, q.shape, 1) < HEAD_DIM
    return jnp.where(low, q, 0.0).astype(BF16), jnp.where(low, 0.0, q).astype(BF16)


def _sb_stages(tl, q_ref, kt_ref, vt_ref, d_sc, tot_sc, carry_sc, acc_sc):
    suffix_ones = _suffix_ones(BLK, BF16)
    q_heads = _split_heads(q_ref)
    carry_sc[...] = jnp.zeros_like(carry_sc)
    acc_sc[...] = jnp.zeros_like(acc_sc)

    def produce(slot, kts, masks):
        work = [(h, t) for t in range(len(kts)) for h in (0, 1)]
        zs = [_dot(q_heads[h], kts[t]) for h, t in work]
        sps = []
        for z, (_, t) in zip(zs, work):
            sp = _softplus2(z)
            sps.append(sp if masks[t] is None else jnp.where(masks[t], sp, 0.0))
        spbs = [sp.astype(BF16) for sp in sps]
        rs = [[_dot(spb[:, c * BLK:(c + 1) * BLK], suffix_ones) for c in range(spb.shape[1] // BLK)]
              for spb in spbs]
        inside = [None, None]
        for k, (h, t) in enumerate(work):
            parts = [None] * len(rs[k])
            for c in reversed(range(len(rs[k]))):
                suf = rs[k][c][:, :BLK]
                parts[c] = suf if inside[h] is None else suf + inside[h]
                tot = rs[k][c][:, BLK:]
                inside[h] = tot if inside[h] is None else inside[h] + tot
            d = (zs[k] - sps[k]) - jnp.concatenate(parts, axis=1)
            if masks[t] is not None:
                d = jnp.where(masks[t], d, NEG)
            d_sc[slot, h * tl.tiles + t, :, :d.shape[1]] = d
        for h in (0, 1):
            tot_sc[slot, h] = inside[h]

    def consume(slot, vts):
        width = vts[0].shape[1]
        probs = {}
        for h in (0, 1):
            shift = jnp.concatenate([carry_sc[h]] * (width // BLK), axis=1)
            for t in range(len(vts)):
                probs[h, t] = jnp.exp2(d_sc[slot, h * tl.tiles + t, :, :width] - shift).astype(BF16)
        pvs = {key: _dot_nt(p, vts[key[1]]) for key, p in probs.items()}
        for h in (0, 1):
            acc_sc[h] = acc_sc[h] + sum(pvs[h, t] for t in range(len(vts)))
            carry_sc[h] = carry_sc[h] + tot_sc[slot, h]

    def tiles(ref, g):
        return [ref[0, :, pl.ds(pl.multiple_of(g * tl.tq + t * tl.tk, tl.tk), tl.tk)]
                for t in reversed(range(tl.tiles))]

    diag_masks = [tl.diag_mask(t, False) for t in reversed(range(tl.tiles))]
    return produce, consume, (lambda g: tiles(kt_ref, g)), (lambda g: tiles(vt_ref, g)), diag_masks


FILL_COLS = 512


def _fox_stages(tl, i, q_ref, kt_ref, vt_ref, ck_ref, va_sc, vb_sc, s_sc, rm_sc, m_sc, acc_sc):
    def augment(vt):
        vf = vt.astype(F32)
        top = lax.broadcasted_iota(jnp.int32, vf.shape, 0) < HEAD_DIM
        return jnp.where(top, vf, 1.0).astype(BF16), jnp.where(top, 1.0, vf).astype(BF16)

    @pl.when(i == 0)
    def _fill():
        def fill(c, carry):
            off = pl.multiple_of(c * FILL_COLS, FILL_COLS)
            va, vb = augment(vt_ref[0, :, pl.ds(off, FILL_COLS)])
            va_sc[:, pl.ds(off, FILL_COLS)] = va
            vb_sc[:, pl.ds(off, FILL_COLS)] = vb
            return carry
        lax.fori_loop(0, va_sc.shape[1] // FILL_COLS, fill, 0)

    q_heads = _split_heads(q_ref)
    m_sc[...] = jnp.full(m_sc.shape, NEG, F32)
    acc_sc[...] = jnp.zeros_like(acc_sc)

    def produce(slot, kts, cks, masks):
        qk = {(h, t): _dot(q_heads[h], kt) for t, kt in enumerate(kts) for h in (0, 1)}
        for h in (0, 1):
            folded = None
            for t, (ck2, mask) in enumerate(zip(cks, masks)):
                s = qk[h, t] - ck2[h:h + 1, :] * LOG2E
                if mask is not None:
                    s = jnp.where(mask, s, NEG)
                s_sc[slot, h * tl.tiles + t, :, :s.shape[1]] = s
                for c in range(s.shape[1] // BLK):
                    blk = s[:, c * BLK:(c + 1) * BLK]
                    folded = blk if folded is None else jnp.maximum(folded, blk)
            rm_sc[slot, h] = jnp.max(folded, axis=-1, keepdims=True)

    def consume(slot, vs):
        width = vs[0][0].shape[1]
        for h in (0, 1):
            m = m_sc[h]
            m_new = jnp.maximum(m, rm_sc[slot, h])
            pv = sum(_dot_nt(jnp.exp2(s_sc[slot, h * tl.tiles + t, :, :width] - m_new).astype(BF16), vs[t][h])
                     for t in range(len(vs)))
            acc_sc[h] = jnp.exp2(m - m_new) * acc_sc[h] + pv
            m_sc[h] = m_new

    def tiles(ref, g, rows):
        return [ref[rows + (pl.ds(pl.multiple_of(g * tl.tq + t * tl.tk, tl.tk), tl.tk),)]
                for t in range(tl.tiles)]

    def keys(g):
        return tiles(kt_ref, g, (0, slice(None))), tiles(ck_ref, g, (0, 0, slice(None)))

    def values(g):
        return list(zip(tiles(va_sc, g, (slice(None),)), tiles(vb_sc, g, (slice(None),))))

    diag_masks = [tl.diag_mask(t, True) for t in range(tl.tiles)]
    return produce, consume, keys, values, diag_masks, augment


DEAD = 160.0


def _sb_kernel(q_ref, kt_ref, vt_ref, ktm_ref, vtm_ref, o_ref, d_sc, tot_sc, carry_sc, acc_sc, *, tl):
    i = pl.program_id(2)
    produce, consume, keys, values, diag_masks = _sb_stages(tl, q_ref, kt_ref, vt_ref, d_sc, tot_sc, carry_sc, acc_sc)
    produce(0, keys(i), diag_masks)

    def alive():
        return (jnp.min(carry_sc[...]) < DEAD).astype(jnp.int32)

    def cond(state):
        j, go = state
        return (j < i) & (go > 0)

    def body(state):
        j, _ = state
        slot = j % 2
        consume(slot, values(i - j))
        go = alive()

        @pl.when(go > 0)
        def _next():
            produce(1 - slot, keys(i - 1 - j), [None] * tl.tiles)

        return j + 1, go

    j_end, go = lax.while_loop(cond, body, (jnp.int32(0), jnp.int32(1)))

    @pl.when(go > 0)
    def _tail():
        consume(j_end % 2, values(i - j_end))
        produce(0, [ktm_ref[...]], [tl.meta_mask()])
        consume(0, [vtm_ref[...]])

    low = lax.broadcasted_iota(jnp.int32, (tl.tq, LANES), 1) < HEAD_DIM
    o_ref[0] = jnp.where(low, acc_sc[0], acc_sc[1])


def _fox_kernel(q_ref, kt_ref, vt_ref, ktm_ref, vtm_ref, ck_ref, ckm_ref, o_ref,
                va_sc, vb_sc, s_sc, rm_sc, m_sc, acc_sc, knorm_sc, *, tl):
    i = pl.program_id(2)
    produce, consume, keys, values, diag_masks, augment = _fox_stages(
        tl, i, q_ref, kt_ref, vt_ref, ck_ref, va_sc, vb_sc, s_sc, rm_sc, m_sc, acc_sc)

    def head_sumsq(kt):
        sq = kt.astype(F32)
        sq = sq * sq
        return jnp.concatenate([jnp.sum(sq[:HEAD_DIM], axis=0, keepdims=True),
                                jnp.sum(sq[HEAD_DIM:], axis=0, keepdims=True)], axis=0)

    @pl.when(i == 0)
    def _key_norms():
        def chunk(c, best):
            off = pl.multiple_of(c * FILL_COLS, FILL_COLS)
            return jnp.maximum(best, head_sumsq(kt_ref[0, :, pl.ds(off, FILL_COLS)]))
        best = lax.fori_loop(0, kt_ref.shape[2] // FILL_COLS, chunk, jnp.zeros((2, FILL_COLS), F32))
        largest = jnp.maximum(jnp.max(best, axis=-1, keepdims=True),
                              jnp.max(head_sumsq(ktm_ref[...]), axis=-1, keepdims=True))
        knorm_sc[...] = jnp.sqrt(largest)

    qf = q_ref[0].astype(F32)
    qsq = qf * qf
    low_q = lax.broadcasted_iota(jnp.int32, qsq.shape, 1) < HEAD_DIM
    q_norms = [jnp.sqrt(jnp.sum(jnp.where(low_q, qsq, 0.0), axis=-1, keepdims=True)),
               jnp.sqrt(jnp.sum(jnp.where(low_q, 0.0, qsq), axis=-1, keepdims=True))]

    def alive(g):
        last = ck_ref[0, 0, :, pl.ds(pl.multiple_of(g * tl.tq + tl.tq - BLK, BLK), BLK)][:, BLK - 1:BLK]
        gap = [q_norms[h] * knorm_sc[h:h + 1, :] - last[h:h + 1, :] * LOG2E - m_sc[h] for h in (0, 1)]
        return (jnp.max(jnp.maximum(gap[0], gap[1])) > -DEAD).astype(jnp.int32)

    produce(0, *keys(i), diag_masks)

    def cond(state):
        j, go = state
        return (j < i) & (go > 0)

    def body(state):
        j, _ = state
        slot = j % 2
        consume(slot, values(i - j))
        go = alive(i - 1 - j)

        @pl.when(go > 0)
        def _next():
            produce(1 - slot, *keys(i - 1 - j), [None] * tl.tiles)

        return j + 1, go

    j_end, go = lax.while_loop(cond, body, (jnp.int32(0), jnp.int32(1)))

    @pl.when(go > 0)
    def _tail():
        consume(j_end % 2, values(i - j_end))
        produce(0, [ktm_ref[...]], [ckm_ref[0]], [tl.meta_mask()])
        consume(0, [augment(vtm_ref[...])])

    low = lax.broadcasted_iota(jnp.int32, (tl.tq, LANES), 1) < HEAD_DIM
    a0, a1 = acc_sc[0], acc_sc[1]
    o_ref[0] = jnp.where(low, a0 / pltpu.roll(a0, HEAD_DIM, 1), a1 / pltpu.roll(a1, HEAD_DIM, 1))


def _attn_call(qb, ktb, vtb, ktm, vtm, ck, ckm, n_pairs):
    n_batch, seq, d = qb.shape

    def specs(tl, off):
        return [pl.BlockSpec((1, tl.tq, LANES), lambda b, p, i: (b, i, p + off)),
                pl.BlockSpec((1, LANES, seq), lambda b, p, i: (b, p + off, 0)),
                pl.BlockSpec((1, LANES, seq), lambda b, p, i: (b, p + off, 0)),
                pl.BlockSpec((LANES, BLK), lambda b, p, i: (p + off, 0)),
                pl.BlockSpec((LANES, BLK), lambda b, p, i: (p + off, 0))]

    def common(tl):
        return dict(
            grid=(n_batch, n_pairs, seq // tl.tq),
            out_specs=pl.BlockSpec((1, tl.tq, LANES), lambda b, p, i: (b, i, p)),
            out_shape=jax.ShapeDtypeStruct((n_batch, seq, n_pairs * LANES), F32),
            compiler_params=_params(("arbitrary", "arbitrary", "arbitrary")))

    staged = lambda tl: pltpu.VMEM((2, 2 * tl.tiles, tl.tq, tl.tk), F32)
    state = lambda tl, lanes: pltpu.VMEM((2, tl.tq, lanes), F32)
    tl = SB_TILING
    o_sb = pl.pallas_call(
        functools.partial(_sb_kernel, tl=tl), in_specs=specs(tl, 0),
        scratch_shapes=[staged(tl), pltpu.VMEM((2, 2, tl.tq, LANES), F32), state(tl, LANES), state(tl, LANES)],
        name="sb_attn", **common(tl),
    )(qb, ktb, vtb, ktm, vtm)
    tl = FOX_TILING
    o_fx = pl.pallas_call(
        functools.partial(_fox_kernel, tl=tl),
        in_specs=specs(tl, n_pairs) + [pl.BlockSpec((1, 1, 2, seq), lambda b, p, i: (b, p, 0, 0)),
                                       pl.BlockSpec((1, 2, LANES), lambda b, p, i: (p, 0, 0))],
        scratch_shapes=[pltpu.VMEM((LANES, seq), BF16), pltpu.VMEM((LANES, seq), BF16),
                        staged(tl), pltpu.VMEM((2, 2, tl.tq, 1), F32), state(tl, 1), state(tl, LANES),
                        pltpu.VMEM((2, 1), F32)],
        name="fox_attn", **common(tl),
    )(qb, ktb, vtb, ktm, vtm, ck, ckm)
    return o_sb, o_fx


PAGES_PER_STEP = 16


def _decode_kernel(pt_ref, q_ref, kn_ref, vn_ref, lfn_ref, *refs, n_pages, h_sb):
    pp = PAGES_PER_STEP
    k_refs = refs[:pp]
    v_refs = refs[pp:2 * pp]
    lf_refs = refs[2 * pp:3 * pp]
    o_ref = refs[3 * pp]
    qb_sc, z_sc, lf_sc, w_sc, t_sc, acc_sc = refs[3 * pp + 1:]
    ph = pl.program_id(1)
    j = pl.program_id(2)
    n_steps = n_pages // pp
    n_heads, hd, page = qb_sc.shape

    @pl.when((ph == 0) & (j == 0))
    def _spread_q():
        for h in range(n_heads):
            qb_sc[h] = jnp.broadcast_to(q_ref[0, h], (hd, page))

    @pl.when(ph == 0)
    def _scores():
        for s in range(pp):
            r = j * pp + s
            for h in range(n_heads):
                z_sc[r, pl.ds(h, 1), :] = jnp.sum(k_refs[s][0, 0, h] * qb_sc[h], axis=0, keepdims=True)
            lf_sc[r] = lf_refs[s][0, 0]

    @pl.when((ph == 0) & (j == n_steps - 1))
    def _weights():
        z = z_sc[...]
        z_sb = z[:, :h_sb, :]
        z_fx = z[:, h_sb:, :]
        sp = _softplus2(z_sb)
        x = jnp.concatenate([sp, lf_sc[...] * LOG2E], axis=1).reshape(n_pages * n_heads, page)
        r = _dot_exact(x, _suffix_ones(page, F32))
        t_sc[...] = r[:, page:].reshape(n_pages, n_heads, page)

        def later_pages(n, carry):
            p = n_pages - 1 - n
            w_sc[p] = carry
            return carry + t_sc[p]

        lax.fori_loop(0, n_pages, later_pages, jnp.zeros((n_heads, page), F32))
        d = r[:, :page].reshape(n_pages, n_heads, page) + w_sc[...]
        a = jnp.exp2((z_sb - sp) - d[:, :h_sb, :])
        logit = z_fx + d[:, h_sb:, :] + lfn_ref[0] * LOG2E
        zs = jnp.sum(q_ref[0] * kn_ref[0], axis=1)[h_sb:]
        m = jnp.maximum(jnp.max(jnp.max(logit, axis=0), axis=-1, keepdims=True), zs)
        pw = jnp.exp2(logit - m)
        ps = jnp.exp2(zs - m)
        den = jnp.sum(jnp.sum(pw, axis=0), axis=-1, keepdims=True) + ps
        w_sc[...] = jnp.concatenate([a, pw / den], axis=1)
        pself = ps / den
        first = lax.broadcasted_iota(jnp.int32, (hd, page), 1) == 0
        for h in range(n_heads):
            if h < h_sb:
                acc_sc[h] = jnp.zeros((hd, page), F32)
            else:
                acc_sc[h] = jnp.where(first, pself[h - h_sb:h - h_sb + 1, :] * vn_ref[0, h], 0.0)

    @pl.when(ph == 1)
    def _values():
        ws = [w_sc[j * pp + s] for s in range(pp)]
        for h in range(n_heads):
            tmp = None
            for s in range(pp):
                t = ws[s][h:h + 1, :] * v_refs[s][0, 0, h]
                tmp = t if tmp is None else tmp + t
            acc_sc[h] = acc_sc[h] + tmp

    @pl.when((ph == 1) & (j == n_steps - 1))
    def _finish():
        for h in range(n_heads):
            o_ref[0, h] = jnp.sum(acc_sc[h], axis=-1, keepdims=True)


def _decode_call(page_table, qcol, kcol, vcol, lfn, cache_kt, cache_vt, cache_lft, layer, h_sb):
    n_b, n_pages = page_table.shape
    _, _, n_heads, hd, page = cache_kt.shape
    h_fox = n_heads - h_sb
    pp = PAGES_PER_STEP
    n_steps = n_pages // pp

    def k_map(s):
        return lambda b, ph, j, pt: (layer, pt[b, jnp.where(ph == 0, j, n_steps - 1) * pp + s], 0, 0, 0)

    def v_map(s):
        return lambda b, ph, j, pt: (layer, pt[b, jnp.where(ph == 1, j, 0) * pp + s], 0, 0, 0)

    def lf_map(s):
        return lambda b, ph, j, pt: (layer, pt[b, jnp.where(ph == 0, j, n_steps - 1) * pp + s], 0, 0)

    col_spec = pl.BlockSpec((1, n_heads, hd, 1), lambda b, ph, j, pt: (b, 0, 0, 0))
    page_block = (1, 1, n_heads, hd, page)
    grid_spec = pltpu.PrefetchScalarGridSpec(
        num_scalar_prefetch=1,
        grid=(n_b, 2, n_steps),
        in_specs=[col_spec, col_spec, col_spec, pl.BlockSpec((1, h_fox, 1), lambda b, ph, j, pt: (b, 0, 0))]
                 + [pl.BlockSpec(page_block, k_map(s)) for s in range(pp)]
                 + [pl.BlockSpec(page_block, v_map(s)) for s in range(pp)]
                 + [pl.BlockSpec((1, 1, h_fox, page), lf_map(s)) for s in range(pp)],
        out_specs=col_spec,
        scratch_shapes=[pltpu.VMEM((n_heads, hd, page), F32),
                        pltpu.VMEM((n_pages, n_heads, page), F32),
                        pltpu.VMEM((n_pages, h_fox, page), F32),
                        pltpu.VMEM((n_pages, n_heads, page), F32),
                        pltpu.VMEM((n_pages, n_heads, page), F32),
                        pltpu.VMEM((n_heads, hd, page), F32)],
    )
    return pl.pallas_call(
        functools.partial(_decode_kernel, n_pages=n_pages, h_sb=h_sb),
        grid_spec=grid_spec,
        out_shape=jax.ShapeDtypeStruct((n_b, n_heads, hd, 1), F32),
        compiler_params=_params(("arbitrary", "arbitrary", "arbitrary")),
        name="decode_attn",
    )(page_table, qcol, kcol, vcol, lfn, *([cache_kt] * pp), *([cache_vt] * pp), *([cache_lft] * pp))


FF_CHUNK = 1024


def _mlp_kernel(osb_ref, ofx_ref, x_ref, gcat_ref, wout_ref, gm_ref, wup_ref, wdn_ref, y_ref):
    def norm(t):
        return t * lax.rsqrt(jnp.mean(t * t, axis=-1, keepdims=True) + EPS)

    on = jnp.concatenate([norm(osb_ref[...]), norm(ofx_ref[...])], axis=1) * gcat_ref[...]
    x1 = x_ref[...] + _dot(on.astype(BF16), wout_ref[...])
    h = (norm(x1) * gm_ref[...]).astype(BF16)
    acc = jnp.zeros_like(x1)
    for c in range(wup_ref.shape[1] // FF_CHUNK):
        u = jnp.maximum(_dot(h, wup_ref[:, c * FF_CHUNK:(c + 1) * FF_CHUNK]), 0.0)
        acc = acc + _dot((u * u).astype(BF16), wdn_ref[c * FF_CHUNK:(c + 1) * FF_CHUNK, :])
    y_ref[...] = x1 + acc


def _mlp_call(o_sb, o_fx, x2d, tm, gcat, wout, gm, wup, wdn):
    rows, d = x2d.shape
    row_spec = pl.BlockSpec((tm, d), lambda i: (i, 0))
    half_spec = pl.BlockSpec((tm, d // 2), lambda i: (i, 0))
    return pl.pallas_call(
        _mlp_kernel,
        grid=(rows // tm,),
        in_specs=[half_spec, half_spec, row_spec, _const_spec(gcat.shape), _const_spec(wout.shape),
                  _const_spec(gm.shape), _const_spec(wup.shape), _const_spec(wdn.shape)],
        out_specs=row_spec,
        out_shape=jax.ShapeDtypeStruct((rows, d), F32),
        compiler_params=_params(("arbitrary",)),
        name="merge_mlp",
    )(o_sb, o_fx, x2d, gcat, wout, gm, wup, wdn)


def kernel(x_prompt, x_sample, cache_k, cache_v, cache_logf, page_table, meta_tokens, norm_attn, w_in, b_forget, q_norm, k_norm, out_norm_sb, out_norm_fox, w_out, norm_mlp, w_up, w_down):
    n_batch, seq, d = x_prompt.shape
    n_dec = x_sample.shape[0]
    depth = w_in.shape[0]
    w_sb = d // 2
    h_fox = b_forget.shape[1]
    h_sb = w_sb // HEAD_DIM
    n_heads = h_sb + h_fox
    n_pairs = w_sb // LANES

    r = lax.broadcasted_iota(jnp.int32, (w_sb, w_sb), 0) // HEAD_DIM
    c = lax.broadcasted_iota(jnp.int32, (w_sb, w_sb), 1) // HEAD_DIM
    grp = (r == c).astype(BF16)

    cache_kt = jnp.transpose(cache_k, (0, 1, 3, 4, 2))
    cache_vt = jnp.transpose(cache_v, (0, 1, 3, 4, 2))
    cache_lft = jnp.transpose(cache_logf, (0, 1, 3, 2))

    xp = x_prompt.reshape(n_batch * seq, d)
    xs = x_sample.reshape(n_dec, d)
    pad_rows = lambda a: jnp.pad(a, ((0, BLK - a.shape[0]), (0, 0)))
    outs = [[] for _ in range(6)]
    for l in range(depth):
        wt = w_in[l].T
        rows = lambda a, b: wt[a:b]
        wq = jnp.concatenate([rows(0, w_sb), rows(3 * w_sb, 4 * w_sb)], axis=0).T.astype(BF16)
        wkvf = jnp.concatenate([rows(w_sb, 2 * w_sb), rows(4 * w_sb, 5 * w_sb),
                                rows(2 * w_sb, 3 * w_sb), rows(5 * w_sb, 6 * w_sb),
                                rows(6 * w_sb, 6 * w_sb + h_fox),
                                jnp.zeros((16 - h_fox, d), w_in.dtype)], axis=0).astype(BF16)
        bf = b_forget[l].reshape(h_fox, 1)
        ga = norm_attn[l].reshape(1, d)
        gq = jnp.tile(q_norm[l], h_fox).reshape(1, w_sb)
        gk = jnp.tile(k_norm[l], h_fox).reshape(w_sb, 1)
        proj = lambda x2d, nb, tm: _proj_call(x2d, nb, tm, ga, wq, wkvf, bf, gq, gk, grp)
        gcat = jnp.concatenate([out_norm_sb[l], out_norm_fox[l]]).reshape(1, d)
        gm = norm_mlp[l].reshape(1, d)
        wo = w_out[l].astype(BF16)
        wu = w_up[l].astype(BF16)
        wd = w_down[l].astype(BF16)

        _, ktm, vtm, kt32_m, vt32_m, lft_m = proj(pad_rows(meta_tokens.astype(x_prompt.dtype)), 1, BLK)
        qb, ktb, vtb, kt32, vt32, lft = proj(xp, n_batch, 512)
        lft_meta = jnp.where(jnp.arange(BLK)[None, :] < N_META, lft_m[0], 0.0)
        c_main, c_meta = _cumsum_call(lft, lft_meta)
        qb3 = qb.reshape(n_batch, seq, d)
        o_sb, o_fx = _attn_call(qb3, ktb, vtb, ktm[0], vtm[0], c_main.reshape(n_batch, n_pairs, 2, seq),
                                c_meta.reshape(n_pairs, 2, LANES), n_pairs)
        xp = _mlp_call(o_sb.reshape(n_batch * seq, w_sb), o_fx.reshape(n_batch * seq, w_sb), xp, 512,
                       gcat, wo, gm, wu, wd)

        def with_meta(main, meta):
            m = jnp.broadcast_to(meta[:, :, :N_META], (n_batch,) + meta.shape[1:2] + (N_META,))
            return jnp.concatenate([m, main], axis=-1)

        tok_major = lambda a: jnp.transpose(a.reshape(n_batch, n_heads, HEAD_DIM, N_META + seq), (0, 3, 1, 2))
        outs[0].append(tok_major(with_meta(kt32, kt32_m)))
        outs[1].append(tok_major(with_meta(vt32, vt32_m)))
        outs[2].append(jnp.transpose(with_meta(lft, lft_m), (0, 2, 1)))

        qb_s, _, _, kt32_s, vt32_s, lft_s = proj(pad_rows(xs), 1, BLK)
        k_s = kt32_s[0, :, :n_dec].T.reshape(n_dec, n_heads, HEAD_DIM)
        v_s = vt32_s[0, :, :n_dec].T.reshape(n_dec, n_heads, HEAD_DIM)
        lf_s = lft_s[0, :, :n_dec].T
        qcol = qb_s[:n_dec].astype(F32).reshape(n_dec, n_heads, HEAD_DIM, 1)
        o_s = _decode_call(page_table, qcol, k_s[..., None], v_s[..., None], lf_s[..., None],
                           cache_kt, cache_vt, cache_lft, l, h_sb)
        o_s = o_s.reshape(n_dec, d)
        xs = _mlp_call(o_s[:, :w_sb], o_s[:, w_sb:], xs, n_dec, gcat, wo, gm, wu, wd)
        outs[3].append(k_s.reshape(n_dec, 1, n_heads, HEAD_DIM))
        outs[4].append(v_s.reshape(n_dec, 1, n_heads, HEAD_DIM))
        outs[5].append(lf_s.reshape(n_dec, 1, h_fox))

    y_prompt = xp.reshape(n_batch, seq, d)
    y_sample = xs.reshape(n_dec, 1, d)
    return (y_prompt, y_sample) + tuple(jnp.stack(o) for o in outs)
```

```python
import functools
import math

import jax
import jax.numpy as jnp
from jax import lax
from jax.experimental import pallas as pl
from jax.experimental.pallas import tpu as pltpu

F32 = jnp.float32
BF16 = jnp.bfloat16

HEAD_DIM = 64
N_META = 16
EPS = 1e-6
NEG = -1e30
LANES = 128
BLK = 128
VMEM_LIMIT = 56 * 1024 * 1024


def _params(sem):
    return pltpu.CompilerParams(dimension_semantics=sem, vmem_limit_bytes=VMEM_LIMIT)


def _dot(a, b):
    return jnp.dot(a, b, preferred_element_type=F32)


def _dot_nt(a, b):
    return lax.dot_general(a, b, (((1,), (1,)), ((), ())), preferred_element_type=F32)


def _dot_exact(a, b):
    return jnp.dot(a, b, preferred_element_type=F32, precision=lax.Precision.HIGHEST)


LOG2E = math.log2(math.e)
Q_SCALE = LOG2E / math.sqrt(HEAD_DIM)


def _softplus2(u):
    return jnp.maximum(u, 0.0) + jnp.log2(1.0 + jnp.exp2(-jnp.abs(u)))


def _const_spec(shape):
    nd = len(shape)
    return pl.BlockSpec(shape, lambda *_: (0,) * nd, pipeline_mode=pl.Buffered(1))


def _suffix_ones(n, dtype):
    r = lax.broadcasted_iota(jnp.int32, (n, 2 * n), 0)
    c = lax.broadcasted_iota(jnp.int32, (n, 2 * n), 1)
    return jnp.where((r > c) | (c >= n), 1.0, 0.0).astype(dtype)


def _proj_kernel(x_ref, ga_ref, wq_ref, wkvf_ref, bf_ref, gq_ref, gk_ref, grp_ref,
                 qb_ref, ktb_ref, vtb_ref, kt32_ref, vt32_ref, lft_ref):
    x = x_ref[...]
    d = x.shape[1]
    w_fox = d // 2
    n_fox = w_fox // HEAD_DIM
    tm = x.shape[0]
    inv = lax.rsqrt(jnp.mean(x * x, axis=-1, keepdims=True) + EPS)
    h = (x * inv * ga_ref[...]).astype(BF16)
    scale = Q_SCALE

    q = _dot(h, wq_ref[...])
    q_fx = q[:, w_fox:]
    sq = q_fx * q_fx
    hi = sq.astype(BF16)
    lo = (sq - hi.astype(F32)).astype(BF16)
    grp = grp_ref[...]
    ms = (_dot(hi, grp) + _dot(lo, grp)) * (1.0 / HEAD_DIM)
    q_fx = q_fx * lax.rsqrt(ms + EPS) * gq_ref[...] * scale
    qb_ref[...] = jnp.concatenate([q[:, :w_fox] * scale, q_fx], axis=1).astype(BF16)

    kvf = _dot_nt(wkvf_ref[...], h)
    k_fx = kvf[w_fox:d].reshape(n_fox, HEAD_DIM, tm)
    k_fx = k_fx * lax.rsqrt(jnp.mean(k_fx * k_fx, axis=1, keepdims=True) + EPS)
    k_fx = k_fx.reshape(w_fox, tm) * gk_ref[...]
    kt = jnp.concatenate([kvf[:w_fox], k_fx], axis=0)
    vt = kvf[d:2 * d]
    ktb_ref[0] = kt.astype(BF16)
    vtb_ref[0] = vt.astype(BF16)
    kt32_ref[0] = kt
    vt32_ref[0] = vt

    y = kvf[2 * d:2 * d + n_fox] + bf_ref[...]
    lft_ref[0] = jnp.minimum(y, 0.0) - jnp.log(1.0 + jnp.exp(-jnp.abs(y)))


def _proj_call(x2d, n_batch, tm, ga, wq, wkvf, bf, gq, gk, grp):
    rows, d = x2d.shape
    seq = rows // n_batch
    per = seq // tm
    n_f = bf.shape[0]
    row_spec = pl.BlockSpec((tm, d), lambda i: (i, 0))
    t_spec = lambda r: pl.BlockSpec((1, r, tm), lambda i: (i // per, 0, i % per))
    t_shape = lambda r, dt: jax.ShapeDtypeStruct((n_batch, r, seq), dt)
    return pl.pallas_call(
        _proj_kernel,
        grid=(rows // tm,),
        in_specs=[row_spec] + [_const_spec(a.shape) for a in (ga, wq, wkvf, bf, gq, gk, grp)],
        out_specs=[row_spec, t_spec(d), t_spec(d), t_spec(d), t_spec(d), t_spec(n_f)],
        out_shape=[jax.ShapeDtypeStruct((rows, d), BF16), t_shape(d, BF16), t_shape(d, BF16),
                   t_shape(d, F32), t_shape(d, F32), t_shape(n_f, F32)],
        compiler_params=_params(("arbitrary",)),
        name="proj",
    )(x2d, ga, wq, wkvf, bf, gq, gk, grp)


def _cumsum_kernel(lf_ref, lfmeta_ref, c_ref, cmeta_ref):
    n_batch, n_h, seq = lf_ref.shape
    r = lax.broadcasted_iota(jnp.int32, (LANES, LANES), 0)
    c = lax.broadcasted_iota(jnp.int32, (LANES, LANES), 1)
    tri = (r <= c).astype(F32)
    cmeta = _dot_exact(lfmeta_ref[...], tri)
    cmeta_ref[...] = cmeta
    base = jnp.broadcast_to(cmeta[:, LANES - 1:LANES], (n_h, LANES))
    for b in range(n_batch):
        def body(i, carry):
            off = pl.multiple_of(i * LANES, LANES)
            pre = _dot_exact(lf_ref[b, :, pl.ds(off, LANES)], tri) + carry
            c_ref[b, :, pl.ds(off, LANES)] = pre
            return jnp.broadcast_to(pre[:, LANES - 1:LANES], (n_h, LANES))
        lax.fori_loop(0, seq // LANES, body, base)


def _cumsum_call(lft_main, lft_meta):
    return pl.pallas_call(
        _cumsum_kernel,
        out_shape=[jax.ShapeDtypeStruct(lft_main.shape, F32), jax.ShapeDtypeStruct(lft_meta.shape, F32)],
        compiler_params=pltpu.CompilerParams(vmem_limit_bytes=VMEM_LIMIT),
        name="cumsum",
    )(lft_main, lft_meta)


class _Tiling:
    def __init__(self, tk, tiles):
        self.tk, self.tiles, self.tq = tk, tiles, tk * tiles

    def diag_mask(self, tile, inclusive):
        row = lax.broadcasted_iota(jnp.int32, (self.tq, self.tk), 0)
        col = lax.broadcasted_iota(jnp.int32, (self.tq, self.tk), 1) + tile * self.tk
        return (col <= row) if inclusive else (col < row)

    def meta_mask(self):
        return lax.broadcasted_iota(jnp.int32, (self.tq, BLK), 1) < N_META


SB_TILING = _Tiling(256, 1)
FOX_TILING = _Tiling(256, 2)


def _split_heads(q_ref):
    q = q_ref[0].astype(F32)
    low = lax.broadcasted_iota(jnp.int32, q.shape, 1) < HEAD_DIM
    return jnp.where(low, q, 0.0).astype(BF16), jnp.where(low, 0.0, q).astype(BF16)


def _sb_stages(tl, q_ref, kt_ref, vt_ref, d_sc, tot_sc, carry_sc, acc_sc):
    suffix_ones = _suffix_ones(BLK, BF16)
    q_heads = _split_heads(q_ref)
    carry_sc[...] = jnp.zeros_like(carry_sc)
    acc_sc[...] = jnp.zeros_like(acc_sc)

    def produce(slot, kts, masks):
        work = [(h, t) for t in range(len(kts)) for h in (0, 1)]
        zs = [_dot(q_heads[h], kts[t]) for h, t in work]
        sps = []
        for z, (_, t) in zip(zs, work):
            sp = _softplus2(z)
            sps.append(sp if masks[t] is None else jnp.where(masks[t], sp, 0.0))
        spbs = [sp.astype(BF16) for sp in sps]
        rs = [[_dot(spb[:, c * BLK:(c + 1) * BLK], suffix_ones) for c in range(spb.shape[1] // BLK)]
              for spb in spbs]
        inside = [None, None]
        for k, (h, t) in enumerate(work):
            parts = [None] * len(rs[k])
            for c in reversed(range(len(rs[k]))):
                suf = rs[k][c][:, :BLK]
                parts[c] = suf if inside[h] is None else suf + inside[h]
                tot = rs[k][c][:, BLK:]
                inside[h] = tot if inside[h] is None else inside[h] + tot
            d = (zs[k] - sps[k]) - jnp.concatenate(parts, axis=1)
            if masks[t] is not None:
                d = jnp.where(masks[t], d, NEG)
            d_sc[slot, h * tl.tiles + t, :, :d.shape[1]] = d
        for h in (0, 1):
            tot_sc[slot, h] = inside[h]

    def consume(slot, vts):
        width = vts[0].shape[1]
        probs = {}
        for h in (0, 1):
            shift = jnp.concatenate([carry_sc[h]] * (width // BLK), axis=1)
            for t in range(len(vts)):
                probs[h, t] = jnp.exp2(d_sc[slot, h * tl.tiles + t, :, :width] - shift).astype(BF16)
        pvs = {key: _dot_nt(p, vts[key[1]]) for key, p in probs.items()}
        for h in (0, 1):
            acc_sc[h] = acc_sc[h] + sum(pvs[h, t] for t in range(len(vts)))
            carry_sc[h] = carry_sc[h] + tot_sc[slot, h]

    def tiles(ref, g):
        return [ref[0, :, pl.ds(pl.multiple_of(g * tl.tq + t * tl.tk, tl.tk), tl.tk)]
                for t in reversed(range(tl.tiles))]

    diag_masks = [tl.diag_mask(t, False) for t in reversed(range(tl.tiles))]
    return produce, consume, (lambda g: tiles(kt_ref, g)), (lambda g: tiles(vt_ref, g)), diag_masks


FILL_COLS = 512


def _fox_stages(tl, i, q_ref, kt_ref, vt_ref, ck_ref, va_sc, vb_sc, s_sc, rm_sc, m_sc, acc_sc):
    def augment(vt):
        vf = vt.astype(F32)
        top = lax.broadcasted_iota(jnp.int32, vf.shape, 0) < HEAD_DIM
        return jnp.where(top, vf, 1.0).astype(BF16), jnp.where(top, 1.0, vf).astype(BF16)

    @pl.when(i == 0)
    def _fill():
        def fill(c, carry):
            off = pl.multiple_of(c * FILL_COLS, FILL_COLS)
            va, vb = augment(vt_ref[0, :, pl.ds(off, FILL_COLS)])
            va_sc[:, pl.ds(off, FILL_COLS)] = va
            vb_sc[:, pl.ds(off, FILL_COLS)] = vb
            return carry
        lax.fori_loop(0, va_sc.shape[1] // FILL_COLS, fill, 0)

    q_heads = _split_heads(q_ref)
    m_sc[...] = jnp.full(m_sc.shape, NEG, F32)
    acc_sc[...] = jnp.zeros_like(acc_sc)

    def produce(slot, kts, cks, masks):
        qk = {(h, t): _dot(q_heads[h], kt) for t, kt in enumerate(kts) for h in (0, 1)}
        for h in (0, 1):
            folded = None
            for t, (ck2, mask) in enumerate(zip(cks, masks)):
                s = qk[h, t] - ck2[h:h + 1, :] * LOG2E
                if mask is not None:
                    s = jnp.where(mask, s, NEG)
                s_sc[slot, h * tl.tiles + t, :, :s.shape[1]] = s
                for c in range(s.shape[1] // BLK):
                    blk = s[:, c * BLK:(c + 1) * BLK]
                    folded = blk if folded is None else jnp.maximum(folded, blk)
            rm_sc[slot, h] = jnp.max(folded, axis=-1, keepdims=True)

    def consume(slot, vs):
        width = vs[0][0].shape[1]
        for h in (0, 1):
            m = m_sc[h]
            m_new = jnp.maximum(m, rm_sc[slot, h])
            pv = sum(_dot_nt(jnp.exp2(s_sc[slot, h * tl.tiles + t, :, :width] - m_new).astype(BF16), vs[t][h])
                     for t in range(len(vs)))
            acc_sc[h] = jnp.exp2(m - m_new) * acc_sc[h] + pv
            m_sc[h] = m_new

    def tiles(ref, g, rows):
        return [ref[rows + (pl.ds(pl.multiple_of(g * tl.tq + t * tl.tk, tl.tk), tl.tk),)]
                for t in range(tl.tiles)]

    def keys(g):
        return tiles(kt_ref, g, (0, slice(None))), tiles(ck_ref, g, (0, 0, slice(None)))

    def values(g):
        return list(zip(tiles(va_sc, g, (slice(None),)), tiles(vb_sc, g, (slice(None),))))

    diag_masks = [tl.diag_mask(t, True) for t in range(tl.tiles)]
    return produce, consume, keys, values, diag_masks, augment


DEAD = 160.0


def _sb_kernel(q_ref, kt_ref, vt_ref, ktm_ref, vtm_ref, o_ref, d_sc, tot_sc, carry_sc, acc_sc, *, tl):
    i = pl.program_id(2)
    produce, consume, keys, values, diag_masks = _sb_stages(tl, q_ref, kt_ref, vt_ref, d_sc, tot_sc, carry_sc, acc_sc)
    produce(0, keys(i), diag_masks)

    def alive():
        return (jnp.min(carry_sc[...]) < DEAD).astype(jnp.int32)

    def cond(state):
        j, go = state
        return (j < i) & (go > 0)

    def body(state):
        j, _ = state
        slot = j % 2
        consume(slot, values(i - j))
        go = alive()

        @pl.when(go > 0)
        def _next():
            produce(1 - slot, keys(i - 1 - j), [None] * tl.tiles)

        return j + 1, go

    j_end, go = lax.while_loop(cond, body, (jnp.int32(0), jnp.int32(1)))

    @pl.when(go > 0)
    def _tail():
        consume(j_end % 2, values(i - j_end))
        produce(0, [ktm_ref[...]], [tl.meta_mask()])
        consume(0, [vtm_ref[...]])

    low = lax.broadcasted_iota(jnp.int32, (tl.tq, LANES), 1) < HEAD_DIM
    o_ref[0] = jnp.where(low, acc_sc[0], acc_sc[1])


def _fox_kernel(q_ref, kt_ref, vt_ref, ktm_ref, vtm_ref, ck_ref, ckm_ref, o_ref,
                va_sc, vb_sc, s_sc, rm_sc, m_sc, acc_sc, knorm_sc, *, tl):
    i = pl.program_id(2)
    produce, consume, keys, values, diag_masks, augment = _fox_stages(
        tl, i, q_ref, kt_ref, vt_ref, ck_ref, va_sc, vb_sc, s_sc, rm_sc, m_sc, acc_sc)

    def head_sumsq(kt):
        sq = kt.astype(F32)
        sq = sq * sq
        return jnp.concatenate([jnp.sum(sq[:HEAD_DIM], axis=0, keepdims=True),
                                jnp.sum(sq[HEAD_DIM:], axis=0, keepdims=True)], axis=0)

    @pl.when(i == 0)
    def _key_norms():
        def chunk(c, best):
            off = pl.multiple_of(c * FILL_COLS, FILL_COLS)
            return jnp.maximum(best, head_sumsq(kt_ref[0, :, pl.ds(off, FILL_COLS)]))
        best = lax.fori_loop(0, kt_ref.shape[2] // FILL_COLS, chunk, jnp.zeros((2, FILL_COLS), F32))
        largest = jnp.maximum(jnp.max(best, axis=-1, keepdims=True),
                              jnp.max(head_sumsq(ktm_ref[...]), axis=-1, keepdims=True))
        knorm_sc[...] = jnp.sqrt(largest)

    qf = q_ref[0].astype(F32)
    qsq = qf * qf
    low_q = lax.broadcasted_iota(jnp.int32, qsq.shape, 1) < HEAD_DIM
    q_norms = [jnp.sqrt(jnp.sum(jnp.where(low_q, qsq, 0.0), axis=-1, keepdims=True)),
               jnp.sqrt(jnp.sum(jnp.where(low_q, 0.0, qsq), axis=-1, keepdims=True))]

    def alive(g):
        last = ck_ref[0, 0, :, pl.ds(pl.multiple_of(g * tl.tq + tl.tq - BLK, BLK), BLK)][:, BLK - 1:BLK]
        gap = [q_norms[h] * knorm_sc[h:h + 1, :] - last[h:h + 1, :] * LOG2E - m_sc[h] for h in (0, 1)]
        return (jnp.max(jnp.maximum(gap[0], gap[1])) > -DEAD).astype(jnp.int32)

    produce(0, *keys(i), diag_masks)

    def cond(state):
        j, go = state
        return (j < i) & (go > 0)

    def body(state):
        j, _ = state
        slot = j % 2
        consume(slot, values(i - j))
        produce(1 - slot, *keys(i - 1 - j), [None] * tl.tiles)
        return j + 1, alive(i - 1 - j)

    j_end, go = lax.while_loop(cond, body, (jnp.int32(0), jnp.int32(1)))

    @pl.when(go > 0)
    def _tail():
        consume(j_end % 2, values(i - j_end))
        produce(0, [ktm_ref[...]], [ckm_ref[0]], [tl.meta_mask()])
        consume(0, [augment(vtm_ref[...])])

    low = lax.broadcasted_iota(jnp.int32, (tl.tq, LANES), 1) < HEAD_DIM
    a0, a1 = acc_sc[0], acc_sc[1]
    o_ref[0] = jnp.where(low, a0 / pltpu.roll(a0, HEAD_DIM, 1), a1 / pltpu.roll(a1, HEAD_DIM, 1))


def _attn_call(qb, ktb, vtb, ktm, vtm, ck, ckm, n_pairs):
    n_batch, seq, d = qb.shape

    def specs(tl, off):
        return [pl.BlockSpec((1, tl.tq, LANES), lambda b, p, i: (b, i, p + off)),
                pl.BlockSpec((1, LANES, seq), lambda b, p, i: (b, p + off, 0)),
                pl.BlockSpec((1, LANES, seq), lambda b, p, i: (b, p + off, 0)),
                pl.BlockSpec((LANES, BLK), lambda b, p, i: (p + off, 0)),
                pl.BlockSpec((LANES, BLK), lambda b, p, i: (p + off, 0))]

    def common(tl):
        return dict(
            grid=(n_batch, n_pairs, seq // tl.tq),
            out_specs=pl.BlockSpec((1, tl.tq, LANES), lambda b, p, i: (b, i, p)),
            out_shape=jax.ShapeDtypeStruct((n_batch, seq, n_pairs * LANES), F32),
            compiler_params=_params(("arbitrary", "arbitrary", "arbitrary")))

    staged = lambda tl: pltpu.VMEM((2, 2 * tl.tiles, tl.tq, tl.tk), F32)
    state = lambda tl, lanes: pltpu.VMEM((2, tl.tq, lanes), F32)
    tl = SB_TILING
    o_sb = pl.pallas_call(
        functools.partial(_sb_kernel, tl=tl), in_specs=specs(tl, 0),
        scratch_shapes=[staged(tl), pltpu.VMEM((2, 2, tl.tq, LANES), F32), state(tl, LANES), state(tl, LANES)],
        name="sb_attn", **common(tl),
    )(qb, ktb, vtb, ktm, vtm)
    tl = FOX_TILING
    o_fx = pl.pallas_call(
        functools.partial(_fox_kernel, tl=tl),
        in_specs=specs(tl, n_pairs) + [pl.BlockSpec((1, 1, 2, seq), lambda b, p, i: (b, p, 0, 0)),
                                       pl.BlockSpec((1, 2, LANES), lambda b, p, i: (p, 0, 0))],
        scratch_shapes=[pltpu.VMEM((LANES, seq), BF16), pltpu.VMEM((LANES, seq), BF16),
                        staged(tl), pltpu.VMEM((2, 2, tl.tq, 1), F32), state(tl, 1), state(tl, LANES),
                        pltpu.VMEM((2, 1), F32)],
        name="fox_attn", **common(tl),
    )(qb, ktb, vtb, ktm, vtm, ck, ckm)
    return o_sb, o_fx


PAGES_PER_STEP = 16


def _decode_kernel(pt_ref, q_ref, kn_ref, vn_ref, lfn_ref, *refs, n_seq, n_pages, h_sb):
    pp = PAGES_PER_STEP
    k_refs = refs[:pp]
    v_refs = refs[pp:2 * pp]
    lf_refs = refs[2 * pp:3 * pp]
    o_ref = refs[3 * pp]
    qb_sc, z_sc, lf_sc, w_sc, t_sc, acc_sc = refs[3 * pp + 1:]
    b = pl.program_id(0)
    j = pl.program_id(1)
    last = n_pages // pp - 1
    n_heads, hd, page = qb_sc.shape
    scoring = b < n_seq
    summing = b >= 1

    @pl.when(scoring & (j == 0))
    def _spread_q():
        for h in range(n_heads):
            qb_sc[h] = jnp.broadcast_to(q_ref[0, h], (hd, page))

    @pl.when(scoring)
    def _scores():
        for s in range(pp):
            r = j * pp + s
            for h in range(n_heads):
                z_sc[r, pl.ds(h, 1), :] = jnp.sum(k_refs[s][0, 0, h] * qb_sc[h], axis=0, keepdims=True)
            lf_sc[r] = lf_refs[s][0, 0]

    @pl.when(summing)
    def _values():
        slot = (b - 1) % 2
        ws = [w_sc[slot, j * pp + s] for s in range(pp)]
        for h in range(n_heads):
            tmp = None
            for s in range(pp):
                t = ws[s][h:h + 1, :] * v_refs[s][0, 0, h]
                tmp = t if tmp is None else tmp + t
            acc_sc[h] = acc_sc[h] + tmp

    @pl.when(summing & (j == last))
    def _finish():
        for h in range(n_heads):
            o_ref[0, h] = jnp.sum(acc_sc[h], axis=-1, keepdims=True)

    @pl.when(scoring & (j == last))
    def _weights():
        slot = b % 2
        z = z_sc[...]
        z_sb = z[:, :h_sb, :]
        z_fx = z[:, h_sb:, :]
        sp = _softplus2(z_sb)
        x = jnp.concatenate([sp, lf_sc[...] * LOG2E], axis=1).reshape(n_pages * n_heads, page)
        r = _dot_exact(x, _suffix_ones(page, F32))
        t_sc[...] = r[:, page:].reshape(n_pages, n_heads, page)

        def later_pages(n, carry):
            p = n_pages - 1 - n
            w_sc[slot, p] = carry
            return carry + t_sc[p]

        lax.fori_loop(0, n_pages, later_pages, jnp.zeros((n_heads, page), F32))
        d = r[:, :page].reshape(n_pages, n_heads, page) + w_sc[slot]
        a = jnp.exp2((z_sb - sp) - d[:, :h_sb, :])
        logit = z_fx + d[:, h_sb:, :] + lfn_ref[0] * LOG2E
        zs = jnp.sum(q_ref[0] * kn_ref[0], axis=1)[h_sb:]
        m = jnp.maximum(jnp.max(jnp.max(logit, axis=0), axis=-1, keepdims=True), zs)
        pw = jnp.exp2(logit - m)
        ps = jnp.exp2(zs - m)
        den = jnp.sum(jnp.sum(pw, axis=0), axis=-1, keepdims=True) + ps
        w_sc[slot] = jnp.concatenate([a, pw / den], axis=1)
        pself = ps / den
        first = lax.broadcasted_iota(jnp.int32, (hd, page), 1) == 0
        for h in range(n_heads):
            if h < h_sb:
                acc_sc[h] = jnp.zeros((hd, page), F32)
            else:
                acc_sc[h] = jnp.where(first, pself[h - h_sb:h - h_sb + 1, :] * vn_ref[0, h], 0.0)


def _decode_call(page_table, qcol, kcol, vcol, lfn, cache_kt, cache_vt, cache_lft, layer, h_sb):
    n_b, n_pages = page_table.shape
    _, _, n_heads, hd, page = cache_kt.shape
    h_fox = n_heads - h_sb
    pp = PAGES_PER_STEP
    last = n_pages // pp - 1
    cur = lambda b: jnp.minimum(b, n_b - 1)
    prev = lambda b: jnp.maximum(b - 1, 0)

    def k_map(s):
        return lambda b, j, pt: (layer, pt[cur(b), jnp.where(b < n_b, j, last) * pp + s], 0, 0, 0)

    def v_map(s):
        return lambda b, j, pt: (layer, pt[prev(b), jnp.where(b >= 1, j, 0) * pp + s], 0, 0, 0)

    def lf_map(s):
        return lambda b, j, pt: (layer, pt[cur(b), jnp.where(b < n_b, j, last) * pp + s], 0, 0)

    cur_col = pl.BlockSpec((1, n_heads, hd, 1), lambda b, j, pt: (cur(b), 0, 0, 0))
    page_block = (1, 1, n_heads, hd, page)
    grid_spec = pltpu.PrefetchScalarGridSpec(
        num_scalar_prefetch=1,
        grid=(n_b + 1, last + 1),
        in_specs=[cur_col, cur_col, cur_col, pl.BlockSpec((1, h_fox, 1), lambda b, j, pt: (cur(b), 0, 0))]
                 + [pl.BlockSpec(page_block, k_map(s)) for s in range(pp)]
                 + [pl.BlockSpec(page_block, v_map(s)) for s in range(pp)]
                 + [pl.BlockSpec((1, 1, h_fox, page), lf_map(s)) for s in range(pp)],
        out_specs=pl.BlockSpec((1, n_heads, hd, 1), lambda b, j, pt: (prev(b), 0, 0, 0)),
        scratch_shapes=[pltpu.VMEM((n_heads, hd, page), F32),
                        pltpu.VMEM((n_pages, n_heads, page), F32),
                        pltpu.VMEM((n_pages, h_fox, page), F32),
                        pltpu.VMEM((2, n_pages, n_heads, page), F32),
                        pltpu.VMEM((n_pages, n_heads, page), F32),
                        pltpu.VMEM((n_heads, hd, page), F32)],
    )
    return pl.pallas_call(
        functools.partial(_decode_kernel, n_seq=n_b, n_pages=n_pages, h_sb=h_sb),
        grid_spec=grid_spec,
        out_shape=jax.ShapeDtypeStruct((n_b, n_heads, hd, 1), F32),
        compiler_params=_params(("arbitrary", "arbitrary")),
        name="decode_attn",
    )(page_table, qcol, kcol, vcol, lfn, *([cache_kt] * pp), *([cache_vt] * pp), *([cache_lft] * pp))


FF_CHUNK = 1024


def _mlp_kernel(osb_ref, ofx_ref, x_ref, gcat_ref, wout_ref, gm_ref, wup_ref, wdn_ref, y_ref):
    def norm(t):
        return t * lax.rsqrt(jnp.mean(t * t, axis=-1, keepdims=True) + EPS)

    on = jnp.concatenate([norm(osb_ref[...]), norm(ofx_ref[...])], axis=1) * gcat_ref[...]
    x1 = x_ref[...] + _dot(on.astype(BF16), wout_ref[...])
    h = (norm(x1) * gm_ref[...]).astype(BF16)
    acc = jnp.zeros_like(x1)
    for c in range(wup_ref.shape[1] // FF_CHUNK):
        u = jnp.maximum(_dot(h, wup_ref[:, c * FF_CHUNK:(c + 1) * FF_CHUNK]), 0.0)
        acc = acc + _dot((u * u).astype(BF16), wdn_ref[c * FF_CHUNK:(c + 1) * FF_CHUNK, :])
    y_ref[...] = x1 + acc


def _mlp_call(o_sb, o_fx, x2d, tm, gcat, wout, gm, wup, wdn):
    rows, d = x2d.shape
    row_spec = pl.BlockSpec((tm, d), lambda i: (i, 0))
    half_spec = pl.BlockSpec((tm, d // 2), lambda i: (i, 0))
    return pl.pallas_call(
        _mlp_kernel,
        grid=(rows // tm,),
        in_specs=[half_spec, half_spec, row_spec, _const_spec(gcat.shape), _const_spec(wout.shape),
                  _const_spec(gm.shape), _const_spec(wup.shape), _const_spec(wdn.shape)],
        out_specs=row_spec,
        out_shape=jax.ShapeDtypeStruct((rows, d), F32),
        compiler_params=_params(("arbitrary",)),
        name="merge_mlp",
    )(o_sb, o_fx, x2d, gcat, wout, gm, wup, wdn)


def kernel(x_prompt, x_sample, cache_k, cache_v, cache_logf, page_table, meta_tokens, norm_attn, w_in, b_forget, q_norm, k_norm, out_norm_sb, out_norm_fox, w_out, norm_mlp, w_up, w_down):
    n_batch, seq, d = x_prompt.shape
    n_dec = x_sample.shape[0]
    depth = w_in.shape[0]
    w_sb = d // 2
    h_fox = b_forget.shape[1]
    h_sb = w_sb // HEAD_DIM
    n_heads = h_sb + h_fox
    n_pairs = w_sb // LANES

    r = lax.broadcasted_iota(jnp.int32, (w_sb, w_sb), 0) // HEAD_DIM
    c = lax.broadcasted_iota(jnp.int32, (w_sb, w_sb), 1) // HEAD_DIM
    grp = (r == c).astype(BF16)

    cache_kt = jnp.transpose(cache_k, (0, 1, 3, 4, 2))
    cache_vt = jnp.transpose(cache_v, (0, 1, 3, 4, 2))
    cache_lft = jnp.transpose(cache_logf, (0, 1, 3, 2))

    xp = x_prompt.reshape(n_batch * seq, d)
    xs = x_sample.reshape(n_dec, d)
    pad_rows = lambda a: jnp.pad(a, ((0, BLK - a.shape[0]), (0, 0)))
    outs = [[] for _ in range(6)]
    for l in range(depth):
        wt = w_in[l].T
        rows = lambda a, b: wt[a:b]
        wq = jnp.concatenate([rows(0, w_sb), rows(3 * w_sb, 4 * w_sb)], axis=0).T.astype(BF16)
        wkvf = jnp.concatenate([rows(w_sb, 2 * w_sb), rows(4 * w_sb, 5 * w_sb),
                                rows(2 * w_sb, 3 * w_sb), rows(5 * w_sb, 6 * w_sb),
                                rows(6 * w_sb, 6 * w_sb + h_fox),
                                jnp.zeros((16 - h_fox, d), w_in.dtype)], axis=0).astype(BF16)
        bf = b_forget[l].reshape(h_fox, 1)
        ga = norm_attn[l].reshape(1, d)
        gq = jnp.tile(q_norm[l], h_fox).reshape(1, w_sb)
        gk = jnp.tile(k_norm[l], h_fox).reshape(w_sb, 1)
        proj = lambda x2d, nb, tm: _proj_call(x2d, nb, tm, ga, wq, wkvf, bf, gq, gk, grp)
        gcat = jnp.concatenate([out_norm_sb[l], out_norm_fox[l]]).reshape(1, d)
        gm = norm_mlp[l].reshape(1, d)
        wo = w_out[l].astype(BF16)
        wu = w_up[l].astype(BF16)
        wd = w_down[l].astype(BF16)

        _, ktm, vtm, kt32_m, vt32_m, lft_m = proj(pad_rows(meta_tokens.astype(x_prompt.dtype)), 1, BLK)
        qb, ktb, vtb, kt32, vt32, lft = proj(xp, n_batch, 512)
        lft_meta = jnp.where(jnp.arange(BLK)[None, :] < N_META, lft_m[0], 0.0)
        c_main, c_meta = _cumsum_call(lft, lft_meta)
        qb3 = qb.reshape(n_batch, seq, d)
        o_sb, o_fx = _attn_call(qb3, ktb, vtb, ktm[0], vtm[0], c_main.reshape(n_batch, n_pairs, 2, seq),
                                c_meta.reshape(n_pairs, 2, LANES), n_pairs)
        xp = _mlp_call(o_sb.reshape(n_batch * seq, w_sb), o_fx.reshape(n_batch * seq, w_sb), xp, 512,
                       gcat, wo, gm, wu, wd)

        def with_meta(main, meta):
            m = jnp.broadcast_to(meta[:, :, :N_META], (n_batch,) + meta.shape[1:2] + (N_META,))
            return jnp.concatenate([m, main], axis=-1)

        tok_major = lambda a: jnp.transpose(a.reshape(n_batch, n_heads, HEAD_DIM, N_META + seq), (0, 3, 1, 2))
        outs[0].append(tok_major(with_meta(kt32, kt32_m)))
        outs[1].append(tok_major(with_meta(vt32, vt32_m)))
        outs[2].append(jnp.transpose(with_meta(lft, lft_m), (0, 2, 1)))

        qb_s, _, _, kt32_s, vt32_s, lft_s = proj(pad_rows(xs), 1, BLK)
        k_s = kt32_s[0, :, :n_dec].T.reshape(n_dec, n_heads, HEAD_DIM)
        v_s = vt32_s[0, :, :n_dec].T.reshape(n_dec, n_heads, HEAD_DIM)
        lf_s = lft_s[0, :, :n_dec].T
        qcol = qb_s[:n_dec].astype(F32).reshape(n_dec, n_heads, HEAD_DIM, 1)
        o_s = _decode_call(page_table, qcol, k_s[..., None], v_s[..., None], lf_s[..., None],
                           cache_kt, cache_vt, cache_lft, l, h_sb)
        o_s = o_s.reshape(n_dec, d)
        xs = _mlp_call(o_s[:, :w_sb], o_s[:, w_sb:], xs, n_dec, gcat, wo, gm, wu, wd)
        outs[3].append(k_s.reshape(n_dec, 1, n_heads, HEAD_DIM))
        outs[4].append(v_s.reshape(n_dec, 1, n_heads, HEAD_DIM))
        outs[5].append(lf_s.reshape(n_dec, 1, h_fox))

    y_prompt = xp.reshape(n_batch, seq, d)
    y_sample = xs.reshape(n_dec, 1, d)
    return (y_prompt, y_sample) + tuple(jnp.stack(o) for o in outs)
```

```python
import functools
import math

import jax
import jax.numpy as jnp
from jax import lax
from jax.experimental import pallas as pl
from jax.experimental.pallas import tpu as pltpu

F32 = jnp.float32
BF16 = jnp.bfloat16

HEAD_DIM = 64
N_META = 16
EPS = 1e-6
NEG = -1e30
LANES = 128
BLK = 128
VMEM_LIMIT = 56 * 1024 * 1024


def _params(sem):
    return pltpu.CompilerParams(dimension_semantics=sem, vmem_limit_bytes=VMEM_LIMIT)


def _dot(a, b):
    return jnp.dot(a, b, preferred_element_type=F32)


def _dot_nt(a, b):
    return lax.dot_general(a, b, (((1,), (1,)), ((), ())), preferred_element_type=F32)


def _dot_exact(a, b):
    return jnp.dot(a, b, preferred_element_type=F32, precision=lax.Precision.HIGHEST)


LOG2E = math.log2(math.e)
Q_SCALE = LOG2E / math.sqrt(HEAD_DIM)


def _softplus2(u):
    return jnp.maximum(u, 0.0) + jnp.log2(1.0 + jnp.exp2(-jnp.abs(u)))


def _const_spec(shape):
    nd = len(shape)
    return pl.BlockSpec(shape, lambda *_: (0,) * nd, pipeline_mode=pl.Buffered(1))


def _suffix_ones(n, dtype):
    r = lax.broadcasted_iota(jnp.int32, (n, 2 * n), 0)
    c = lax.broadcasted_iota(jnp.int32, (n, 2 * n), 1)
    return jnp.where((r > c) | (c >= n), 1.0, 0.0).astype(dtype)


def _proj_kernel(x_ref, ga_ref, wq_ref, wkvf_ref, bf_ref, gq_ref, gk_ref, grp_ref,
                 qb_ref, ktb_ref, vtb_ref, kt32_ref, vt32_ref, lft_ref):
    x = x_ref[...]
    d = x.shape[1]
    w_fox = d // 2
    n_fox = w_fox // HEAD_DIM
    tm = x.shape[0]
    inv = lax.rsqrt(jnp.mean(x * x, axis=-1, keepdims=True) + EPS)
    h = (x * inv * ga_ref[...]).astype(BF16)
    scale = Q_SCALE

    q = _dot(h, wq_ref[...])
    q_fx = q[:, w_fox:]
    sq = q_fx * q_fx
    hi = sq.astype(BF16)
    lo = (sq - hi.astype(F32)).astype(BF16)
    grp = grp_ref[...]
    ms = (_dot(hi, grp) + _dot(lo, grp)) * (1.0 / HEAD_DIM)
    q_fx = q_fx * lax.rsqrt(ms + EPS) * gq_ref[...] * scale
    qb_ref[...] = jnp.concatenate([q[:, :w_fox] * scale, q_fx], axis=1).astype(BF16)

    kvf = _dot_nt(wkvf_ref[...], h)
    k_fx = kvf[w_fox:d].reshape(n_fox, HEAD_DIM, tm)
    k_fx = k_fx * lax.rsqrt(jnp.mean(k_fx * k_fx, axis=1, keepdims=True) + EPS)
    k_fx = k_fx.reshape(w_fox, tm) * gk_ref[...]
    kt = jnp.concatenate([kvf[:w_fox], k_fx], axis=0)
    vt = kvf[d:2 * d]
    ktb_ref[0] = kt.astype(BF16)
    vtb_ref[0] = vt.astype(BF16)
    kt32_ref[0] = kt
    vt32_ref[0] = vt

    y = kvf[2 * d:2 * d + n_fox] + bf_ref[...]
    lft_ref[0] = jnp.minimum(y, 0.0) - jnp.log(1.0 + jnp.exp(-jnp.abs(y)))


def _proj_call(x2d, n_batch, tm, ga, wq, wkvf, bf, gq, gk, grp):
    rows, d = x2d.shape
    seq = rows // n_batch
    per = seq // tm
    n_f = bf.shape[0]
    row_spec = pl.BlockSpec((tm, d), lambda i: (i, 0))
    t_spec = lambda r: pl.BlockSpec((1, r, tm), lambda i: (i // per, 0, i % per))
    t_shape = lambda r, dt: jax.ShapeDtypeStruct((n_batch, r, seq), dt)
    return pl.pallas_call(
        _proj_kernel,
        grid=(rows // tm,),
        in_specs=[row_spec] + [_const_spec(a.shape) for a in (ga, wq, wkvf, bf, gq, gk, grp)],
        out_specs=[row_spec, t_spec(d), t_spec(d), t_spec(d), t_spec(d), t_spec(n_f)],
        out_shape=[jax.ShapeDtypeStruct((rows, d), BF16), t_shape(d, BF16), t_shape(d, BF16),
                   t_shape(d, F32), t_shape(d, F32), t_shape(n_f, F32)],
        compiler_params=_params(("arbitrary",)),
        name="proj",
    )(x2d, ga, wq, wkvf, bf, gq, gk, grp)


def _cumsum_kernel(lf_ref, lfmeta_ref, c_ref, cmeta_ref):
    n_batch, n_h, seq = lf_ref.shape
    r = lax.broadcasted_iota(jnp.int32, (LANES, LANES), 0)
    c = lax.broadcasted_iota(jnp.int32, (LANES, LANES), 1)
    tri = (r <= c).astype(F32)
    cmeta = _dot_exact(lfmeta_ref[...], tri)
    cmeta_ref[...] = cmeta
    base = jnp.broadcast_to(cmeta[:, LANES - 1:LANES], (n_h, LANES))

    def body(i, carry):
        off = pl.multiple_of(i * LANES, LANES)
        chunk = lf_ref[:, :, pl.ds(off, LANES)].reshape(n_batch * n_h, LANES)
        pre = _dot_exact(chunk, tri) + carry
        c_ref[:, :, pl.ds(off, LANES)] = pre.reshape(n_batch, n_h, LANES)
        return jnp.broadcast_to(pre[:, LANES - 1:LANES], (n_batch * n_h, LANES))

    lax.fori_loop(0, seq // LANES, body, jnp.concatenate([base] * n_batch, axis=0))


def _cumsum_call(lft_main, lft_meta):
    return pl.pallas_call(
        _cumsum_kernel,
        out_shape=[jax.ShapeDtypeStruct(lft_main.shape, F32), jax.ShapeDtypeStruct(lft_meta.shape, F32)],
        compiler_params=pltpu.CompilerParams(vmem_limit_bytes=VMEM_LIMIT),
        name="cumsum",
    )(lft_main, lft_meta)


class _Tiling:
    def __init__(self, tk, tiles):
        self.tk, self.tiles, self.tq = tk, tiles, tk * tiles

    def diag_mask(self, tile, inclusive):
        row = lax.broadcasted_iota(jnp.int32, (self.tq, self.tk), 0)
        col = lax.broadcasted_iota(jnp.int32, (self.tq, self.tk), 1) + tile * self.tk
        return (col <= row) if inclusive else (col < row)

    def meta_mask(self):
        return lax.broadcasted_iota(jnp.int32, (self.tq, BLK), 1) < N_META


SB_TILING = _Tiling(256, 1)
FOX_TILING = _Tiling(256, 2)


def _split_heads(q_ref):
    q = q_ref[0].astype(F32)
    low = lax.broadcasted_iota(jnp.int32, q.shape, 1) < HEAD_DIM
    return jnp.where(low, q, 0.0).astype(BF16), jnp.where(low, 0.0, q).astype(BF16)


def _sb_stages(tl, q_ref, kt_ref, vt_ref, d_sc, tot_sc, carry_sc, acc_sc):
    suffix_ones = _suffix_ones(BLK, BF16)
    q_heads = _split_heads(q_ref)
    carry_sc[...] = jnp.zeros_like(carry_sc)
    acc_sc[...] = jnp.zeros_like(acc_sc)

    def produce(slot, kts, masks):
        work = [(h, t) for t in range(len(kts)) for h in (0, 1)]
        zs = [_dot(q_heads[h], kts[t]) for h, t in work]
        sps = []
        for z, (_, t) in zip(zs, work):
            sp = _softplus2(z)
            sps.append(sp if masks[t] is None else jnp.where(masks[t], sp, 0.0))
        spbs = [sp.astype(BF16) for sp in sps]
        rs = [[_dot(spb[:, c * BLK:(c + 1) * BLK], suffix_ones) for c in range(spb.shape[1] // BLK)]
              for spb in spbs]
        inside = [None, None]
        for k, (h, t) in enumerate(work):
            parts = [None] * len(rs[k])
            for c in reversed(range(len(rs[k]))):
                suf = rs[k][c][:, :BLK]
                parts[c] = suf if inside[h] is None else suf + inside[h]
                tot = rs[k][c][:, BLK:]
                inside[h] = tot if inside[h] is None else inside[h] + tot
            d = (zs[k] - sps[k]) - jnp.concatenate(parts, axis=1)
            if masks[t] is not None:
                d = jnp.where(masks[t], d, NEG)
            d_sc[slot, h * tl.tiles + t, :, :d.shape[1]] = d
        for h in (0, 1):
            tot_sc[slot, h] = inside[h]

    def consume(slot, vts):
        width = vts[0].shape[1]
        probs = {}
        for h in (0, 1):
            shift = jnp.concatenate([carry_sc[h]] * (width // BLK), axis=1)
            for t in range(len(vts)):
                probs[h, t] = jnp.exp2(d_sc[slot, h * tl.tiles + t, :, :width] - shift).astype(BF16)
        pvs = {key: _dot_nt(p, vts[key[1]]) for key, p in probs.items()}
        for h in (0, 1):
            acc_sc[h] = acc_sc[h] + sum(pvs[h, t] for t in range(len(vts)))
            carry_sc[h] = carry_sc[h] + tot_sc[slot, h]

    def tiles(ref, g):
        return [ref[0, :, pl.ds(pl.multiple_of(g * tl.tq + t * tl.tk, tl.tk), tl.tk)]
                for t in reversed(range(tl.tiles))]

    diag_masks = [tl.diag_mask(t, False) for t in reversed(range(tl.tiles))]
    return produce, consume, (lambda g: tiles(kt_ref, g)), (lambda g: tiles(vt_ref, g)), diag_masks


FILL_COLS = 512


def _fox_stages(tl, i, q_ref, kt_ref, vt_ref, ck_ref, va_sc, vb_sc, s_sc, rm_sc, m_sc, acc_sc):
    def augment(vt):
        vf = vt.astype(F32)
        top = lax.broadcasted_iota(jnp.int32, vf.shape, 0) < HEAD_DIM
        return jnp.where(top, vf, 1.0).astype(BF16), jnp.where(top, 1.0, vf).astype(BF16)

    @pl.when(i == 0)
    def _fill():
        def fill(c, carry):
            off = pl.multiple_of(c * FILL_COLS, FILL_COLS)
            va, vb = augment(vt_ref[0, :, pl.ds(off, FILL_COLS)])
            va_sc[:, pl.ds(off, FILL_COLS)] = va
            vb_sc[:, pl.ds(off, FILL_COLS)] = vb
            return carry
        lax.fori_loop(0, va_sc.shape[1] // FILL_COLS, fill, 0)

    q_heads = _split_heads(q_ref)
    m_sc[...] = jnp.full(m_sc.shape, NEG, F32)
    acc_sc[...] = jnp.zeros_like(acc_sc)

    def produce(slot, kts, cks, masks):
        qk = {(h, t): _dot(q_heads[h], kt) for t, kt in enumerate(kts) for h in (0, 1)}
        for h in (0, 1):
            folded = None
            for t, (ck2, mask) in enumerate(zip(cks, masks)):
                s = qk[h, t] - ck2[h:h + 1, :] * LOG2E
                if mask is not None:
                    s = jnp.where(mask, s, NEG)
                s_sc[slot, h * tl.tiles + t, :, :s.shape[1]] = s
                for c in range(s.shape[1] // BLK):
                    blk = s[:, c * BLK:(c + 1) * BLK]
                    folded = blk if folded is None else jnp.maximum(folded, blk)
            rm_sc[slot, h] = jnp.max(folded, axis=-1, keepdims=True)

    def consume(slot, vs):
        width = vs[0][0].shape[1]
        for h in (0, 1):
            m = m_sc[h]
            m_new = jnp.maximum(m, rm_sc[slot, h])
            pv = sum(_dot_nt(jnp.exp2(s_sc[slot, h * tl.tiles + t, :, :width] - m_new).astype(BF16), vs[t][h])
                     for t in range(len(vs)))
            acc_sc[h] = jnp.exp2(m - m_new) * acc_sc[h] + pv
            m_sc[h] = m_new

    def tiles(ref, g, rows):
        return [ref[rows + (pl.ds(pl.multiple_of(g * tl.tq + t * tl.tk, tl.tk), tl.tk),)]
                for t in range(tl.tiles)]

    def keys(g):
        return tiles(kt_ref, g, (0, slice(None))), tiles(ck_ref, g, (0, 0, slice(None)))

    def values(g):
        return list(zip(tiles(va_sc, g, (slice(None),)), tiles(vb_sc, g, (slice(None),))))

    diag_masks = [tl.diag_mask(t, True) for t in range(tl.tiles)]
    return produce, consume, keys, values, diag_masks, augment


DEAD = 160.0


def _sb_kernel(q_ref, kt_ref, vt_ref, ktm_ref, vtm_ref, o_ref, d_sc, tot_sc, carry_sc, acc_sc, *, tl):
    i = pl.program_id(2)
    produce, consume, keys, values, diag_masks = _sb_stages(tl, q_ref, kt_ref, vt_ref, d_sc, tot_sc, carry_sc, acc_sc)
    produce(0, keys(i), diag_masks)

    def alive():
        return (jnp.min(carry_sc[...]) < DEAD).astype(jnp.int32)

    def cond(state):
        j, go = state
        return (j < i) & (go > 0)

    def body(state):
        j, _ = state
        slot = j % 2
        consume(slot, values(i - j))
        go = alive()

        @pl.when(go > 0)
        def _next():
            produce(1 - slot, keys(i - 1 - j), [None] * tl.tiles)

        return j + 1, go

    j_end, go = lax.while_loop(cond, body, (jnp.int32(0), jnp.int32(1)))

    @pl.when(go > 0)
    def _tail():
        consume(j_end % 2, values(i - j_end))
        produce(0, [ktm_ref[...]], [tl.meta_mask()])
        consume(0, [vtm_ref[...]])

    low = lax.broadcasted_iota(jnp.int32, (tl.tq, LANES), 1) < HEAD_DIM
    o_ref[0] = jnp.where(low, acc_sc[0], acc_sc[1])


def _fox_kernel(q_ref, kt_ref, vt_ref, ktm_ref, vtm_ref, ck_ref, ckm_ref, o_ref,
                va_sc, vb_sc, s_sc, rm_sc, m_sc, acc_sc, knorm_sc, *, tl):
    i = pl.program_id(2)
    produce, consume, keys, values, diag_masks, augment = _fox_stages(
        tl, i, q_ref, kt_ref, vt_ref, ck_ref, va_sc, vb_sc, s_sc, rm_sc, m_sc, acc_sc)

    def head_sumsq(kt):
        sq = kt.astype(F32)
        sq = sq * sq
        return jnp.concatenate([jnp.sum(sq[:HEAD_DIM], axis=0, keepdims=True),
                                jnp.sum(sq[HEAD_DIM:], axis=0, keepdims=True)], axis=0)

    @pl.when(i == 0)
    def _key_norms():
        def chunk(c, best):
            off = pl.multiple_of(c * FILL_COLS, FILL_COLS)
            return jnp.maximum(best, head_sumsq(kt_ref[0, :, pl.ds(off, FILL_COLS)]))
        best = lax.fori_loop(0, kt_ref.shape[2] // FILL_COLS, chunk, jnp.zeros((2, FILL_COLS), F32))
        largest = jnp.maximum(jnp.max(best, axis=-1, keepdims=True),
                              jnp.max(head_sumsq(ktm_ref[...]), axis=-1, keepdims=True))
        knorm_sc[...] = jnp.sqrt(largest)

    qf = q_ref[0].astype(F32)
    qsq = qf * qf
    low_q = lax.broadcasted_iota(jnp.int32, qsq.shape, 1) < HEAD_DIM
    q_norms = [jnp.sqrt(jnp.sum(jnp.where(low_q, qsq, 0.0), axis=-1, keepdims=True)),
               jnp.sqrt(jnp.sum(jnp.where(low_q, 0.0, qsq), axis=-1, keepdims=True))]

    def alive(g):
        last = ck_ref[0, 0, :, pl.ds(pl.multiple_of(g * tl.tq + tl.tq - BLK, BLK), BLK)][:, BLK - 1:BLK]
        gap = [q_norms[h] * knorm_sc[h:h + 1, :] - last[h:h + 1, :] * LOG2E - m_sc[h] for h in (0, 1)]
        return (jnp.max(jnp.maximum(gap[0], gap[1])) > -DEAD).astype(jnp.int32)

    produce(0, *keys(i), diag_masks)

    def cond(state):
        j, go = state
        return (j < i) & (go > 0)

    def body(state):
        j, _ = state
        slot = j % 2
        consume(slot, values(i - j))
        produce(1 - slot, *keys(i - 1 - j), [None] * tl.tiles)
        return j + 1, alive(i - 1 - j)

    j_end, go = lax.while_loop(cond, body, (jnp.int32(0), jnp.int32(1)))

    @pl.when(go > 0)
    def _tail():
        consume(j_end % 2, values(i - j_end))
        produce(0, [ktm_ref[...]], [ckm_ref[0]], [tl.meta_mask()])
        consume(0, [augment(vtm_ref[...])])

    low = lax.broadcasted_iota(jnp.int32, (tl.tq, LANES), 1) < HEAD_DIM
    a0, a1 = acc_sc[0], acc_sc[1]
    o_ref[0] = jnp.where(low, a0 / pltpu.roll(a0, HEAD_DIM, 1), a1 / pltpu.roll(a1, HEAD_DIM, 1))


def _attn_call(qb, ktb, vtb, ktm, vtm, ck, ckm, n_pairs):
    n_batch, seq, d = qb.shape

    def specs(tl, off):
        return [pl.BlockSpec((1, tl.tq, LANES), lambda b, p, i: (b, i, p + off)),
                pl.BlockSpec((1, LANES, seq), lambda b, p, i: (b, p + off, 0)),
                pl.BlockSpec((1, LANES, seq), lambda b, p, i: (b, p + off, 0)),
                pl.BlockSpec((LANES, BLK), lambda b, p, i: (p + off, 0)),
                pl.BlockSpec((LANES, BLK), lambda b, p, i: (p + off, 0))]

    def common(tl):
        return dict(
            grid=(n_batch, n_pairs, seq // tl.tq),
            out_specs=pl.BlockSpec((1, tl.tq, LANES), lambda b, p, i: (b, i, p)),
            out_shape=jax.ShapeDtypeStruct((n_batch, seq, n_pairs * LANES), F32),
            compiler_params=_params(("arbitrary", "arbitrary", "arbitrary")))

    staged = lambda tl: pltpu.VMEM((2, 2 * tl.tiles, tl.tq, tl.tk), F32)
    state = lambda tl, lanes: pltpu.VMEM((2, tl.tq, lanes), F32)
    tl = SB_TILING
    o_sb = pl.pallas_call(
        functools.partial(_sb_kernel, tl=tl), in_specs=specs(tl, 0),
        scratch_shapes=[staged(tl), pltpu.VMEM((2, 2, tl.tq, LANES), F32), state(tl, LANES), state(tl, LANES)],
        name="sb_attn", **common(tl),
    )(qb, ktb, vtb, ktm, vtm)
    tl = FOX_TILING
    o_fx = pl.pallas_call(
        functools.partial(_fox_kernel, tl=tl),
        in_specs=specs(tl, n_pairs) + [pl.BlockSpec((1, 1, 2, seq), lambda b, p, i: (b, p, 0, 0)),
                                       pl.BlockSpec((1, 2, LANES), lambda b, p, i: (p, 0, 0))],
        scratch_shapes=[pltpu.VMEM((LANES, seq), BF16), pltpu.VMEM((LANES, seq), BF16),
                        staged(tl), pltpu.VMEM((2, 2, tl.tq, 1), F32), state(tl, 1), state(tl, LANES),
                        pltpu.VMEM((2, 1), F32)],
        name="fox_attn", **common(tl),
    )(qb, ktb, vtb, ktm, vtm, ck, ckm)
    return o_sb, o_fx


PAGES_PER_STEP = 16


def _decode_kernel(pt_ref, q_ref, kn_ref, vn_ref, lfn_ref, *refs, n_seq, n_pages, h_sb):
    pp = PAGES_PER_STEP
    k_refs = refs[:pp]
    v_refs = refs[pp:2 * pp]
    lf_refs = refs[2 * pp:3 * pp]
    o_ref = refs[3 * pp]
    qb_sc, z_sc, lf_sc, w_sc, t_sc, acc_sc = refs[3 * pp + 1:]
    b = pl.program_id(0)
    j = pl.program_id(1)
    last = n_pages // pp - 1
    n_heads, hd, page = qb_sc.shape
    scoring = b < n_seq
    summing = b >= 1

    @pl.when(scoring & (j == 0))
    def _spread_q():
        for h in range(n_heads):
            qb_sc[h] = jnp.broadcast_to(q_ref[0, h], (hd, page))

    @pl.when(scoring)
    def _scores():
        for s in range(pp):
            r = j * pp + s
            for h in range(n_heads):
                z_sc[r, pl.ds(h, 1), :] = jnp.sum(k_refs[s][0, 0, h] * qb_sc[h], axis=0, keepdims=True)
            lf_sc[r] = lf_refs[s][0, 0]

    @pl.when(summing)
    def _values():
        slot = (b - 1) % 2
        ws = [w_sc[slot, j * pp + s] for s in range(pp)]
        for h in range(n_heads):
            tmp = None
            for s in range(pp):
                t = ws[s][h:h + 1, :] * v_refs[s][0, 0, h]
                tmp = t if tmp is None else tmp + t
            acc_sc[h] = acc_sc[h] + tmp

    @pl.when(summing & (j == last))
    def _finish():
        for h in range(n_heads):
            o_ref[0, h] = jnp.sum(acc_sc[h], axis=-1, keepdims=True)

    @pl.when(scoring & (j == last))
    def _weights():
        slot = b % 2
        z = z_sc[...]
        z_sb = z[:, :h_sb, :]
        z_fx = z[:, h_sb:, :]
        sp = _softplus2(z_sb)
        x = jnp.concatenate([sp, lf_sc[...] * LOG2E], axis=1).reshape(n_pages * n_heads, page)
        r = _dot_exact(x, _suffix_ones(page, F32))
        t_sc[...] = r[:, page:].reshape(n_pages, n_heads, page)

        def later_pages(n, carry):
            p = n_pages - 1 - n
            w_sc[slot, p] = carry
            return carry + t_sc[p]

        lax.fori_loop(0, n_pages, later_pages, jnp.zeros((n_heads, page), F32))
        d = r[:, :page].reshape(n_pages, n_heads, page) + w_sc[slot]
        a = jnp.exp2((z_sb - sp) - d[:, :h_sb, :])
        logit = z_fx + d[:, h_sb:, :] + lfn_ref[0] * LOG2E
        zs = jnp.sum(q_ref[0] * kn_ref[0], axis=1)[h_sb:]
        m = jnp.maximum(jnp.max(jnp.max(logit, axis=0), axis=-1, keepdims=True), zs)
        pw = jnp.exp2(logit - m)
        ps = jnp.exp2(zs - m)
        den = jnp.sum(jnp.sum(pw, axis=0), axis=-1, keepdims=True) + ps
        w_sc[slot] = jnp.concatenate([a, pw / den], axis=1)
        pself = ps / den
        first = lax.broadcasted_iota(jnp.int32, (hd, page), 1) == 0
        for h in range(n_heads):
            if h < h_sb:
                acc_sc[h] = jnp.zeros((hd, page), F32)
            else:
                acc_sc[h] = jnp.where(first, pself[h - h_sb:h - h_sb + 1, :] * vn_ref[0, h], 0.0)


def _decode_call(page_table, qcol, kcol, vcol, lfn, cache_kt, cache_vt, cache_lft, layer, h_sb):
    n_b, n_pages = page_table.shape
    _, _, n_heads, hd, page = cache_kt.shape
    h_fox = n_heads - h_sb
    pp = PAGES_PER_STEP
    last = n_pages // pp - 1
    cur = lambda b: jnp.minimum(b, n_b - 1)
    prev = lambda b: jnp.maximum(b - 1, 0)

    def k_map(s):
        return lambda b, j, pt: (layer, pt[cur(b), jnp.where(b < n_b, j, last) * pp + s], 0, 0, 0)

    def v_map(s):
        return lambda b, j, pt: (layer, pt[prev(b), jnp.where(b >= 1, j, 0) * pp + s], 0, 0, 0)

    def lf_map(s):
        return lambda b, j, pt: (layer, pt[cur(b), jnp.where(b < n_b, j, last) * pp + s], 0, 0)

    cur_col = pl.BlockSpec((1, n_heads, hd, 1), lambda b, j, pt: (cur(b), 0, 0, 0))
    page_block = (1, 1, n_heads, hd, page)
    grid_spec = pltpu.PrefetchScalarGridSpec(
        num_scalar_prefetch=1,
        grid=(n_b + 1, last + 1),
        in_specs=[cur_col, cur_col, cur_col, pl.BlockSpec((1, h_fox, 1), lambda b, j, pt: (cur(b), 0, 0))]
                 + [pl.BlockSpec(page_block, k_map(s)) for s in range(pp)]
                 + [pl.BlockSpec(page_block, v_map(s)) for s in range(pp)]
                 + [pl.BlockSpec((1, 1, h_fox, page), lf_map(s)) for s in range(pp)],
        out_specs=pl.BlockSpec((1, n_heads, hd, 1), lambda b, j, pt: (prev(b), 0, 0, 0)),
        scratch_shapes=[pltpu.VMEM((n_heads, hd, page), F32),
                        pltpu.VMEM((n_pages, n_heads, page), F32),
                        pltpu.VMEM((n_pages, h_fox, page), F32),
                        pltpu.VMEM((2, n_pages, n_heads, page), F32),
                        pltpu.VMEM((n_pages, n_heads, page), F32),
                        pltpu.VMEM((n_heads, hd, page), F32)],
    )
    return pl.pallas_call(
        functools.partial(_decode_kernel, n_seq=n_b, n_pages=n_pages, h_sb=h_sb),
        grid_spec=grid_spec,
        out_shape=jax.ShapeDtypeStruct((n_b, n_heads, hd, 1), F32),
        compiler_params=_params(("arbitrary", "arbitrary")),
        name="decode_attn",
    )(page_table, qcol, kcol, vcol, lfn, *([cache_kt] * pp), *([cache_vt] * pp), *([cache_lft] * pp))


FF_CHUNK = 1024


def _mlp_kernel(osb_ref, ofx_ref, x_ref, gcat_ref, wout_ref, gm_ref, wup_ref, wdn_ref, y_ref):
    def norm(t):
        return t * lax.rsqrt(jnp.mean(t * t, axis=-1, keepdims=True) + EPS)

    on = jnp.concatenate([norm(osb_ref[...]), norm(ofx_ref[...])], axis=1) * gcat_ref[...]
    x1 = x_ref[...] + _dot(on.astype(BF16), wout_ref[...])
    h = (norm(x1) * gm_ref[...]).astype(BF16)
    acc = jnp.zeros_like(x1)
    for c in range(wup_ref.shape[1] // FF_CHUNK):
        u = jnp.maximum(_dot(h, wup_ref[:, c * FF_CHUNK:(c + 1) * FF_CHUNK]), 0.0)
        acc = acc + _dot((u * u).astype(BF16), wdn_ref[c * FF_CHUNK:(c + 1) * FF_CHUNK, :])
    y_ref[...] = x1 + acc


def _mlp_call(o_sb, o_fx, x2d, tm, gcat, wout, gm, wup, wdn):
    rows, d = x2d.shape
    row_spec = pl.BlockSpec((tm, d), lambda i: (i, 0))
    half_spec = pl.BlockSpec((tm, d // 2), lambda i: (i, 0))
    return pl.pallas_call(
        _mlp_kernel,
        grid=(rows // tm,),
        in_specs=[half_spec, half_spec, row_spec, _const_spec(gcat.shape), _const_spec(wout.shape),
                  _const_spec(gm.shape), _const_spec(wup.shape), _const_spec(wdn.shape)],
        out_specs=row_spec,
        out_shape=jax.ShapeDtypeStruct((rows, d), F32),
        compiler_params=_params(("arbitrary",)),
        name="merge_mlp",
    )(o_sb, o_fx, x2d, gcat, wout, gm, wup, wdn)


def kernel(x_prompt, x_sample, cache_k, cache_v, cache_logf, page_table, meta_tokens, norm_attn, w_in, b_forget, q_norm, k_norm, out_norm_sb, out_norm_fox, w_out, norm_mlp, w_up, w_down):
    n_batch, seq, d = x_prompt.shape
    n_dec = x_sample.shape[0]
    depth = w_in.shape[0]
    w_sb = d // 2
    h_fox = b_forget.shape[1]
    h_sb = w_sb // HEAD_DIM
    n_heads = h_sb + h_fox
    n_pairs = w_sb // LANES

    r = lax.broadcasted_iota(jnp.int32, (w_sb, w_sb), 0) // HEAD_DIM
    c = lax.broadcasted_iota(jnp.int32, (w_sb, w_sb), 1) // HEAD_DIM
    grp = (r == c).astype(BF16)

    cache_kt = jnp.transpose(cache_k, (0, 1, 3, 4, 2))
    cache_vt = jnp.transpose(cache_v, (0, 1, 3, 4, 2))
    cache_lft = jnp.transpose(cache_logf, (0, 1, 3, 2))

    xp = x_prompt.reshape(n_batch * seq, d)
    xs = x_sample.reshape(n_dec, d)
    pad_rows = lambda a: jnp.pad(a, ((0, BLK - a.shape[0]), (0, 0)))
    outs = [[] for _ in range(6)]
    for l in range(depth):
        wt = w_in[l].T
        rows = lambda a, b: wt[a:b]
        wq = jnp.concatenate([rows(0, w_sb), rows(3 * w_sb, 4 * w_sb)], axis=0).T.astype(BF16)
        wkvf = jnp.concatenate([rows(w_sb, 2 * w_sb), rows(4 * w_sb, 5 * w_sb),
                                rows(2 * w_sb, 3 * w_sb), rows(5 * w_sb, 6 * w_sb),
                                rows(6 * w_sb, 6 * w_sb + h_fox),
                                jnp.zeros((16 - h_fox, d), w_in.dtype)], axis=0).astype(BF16)
        bf = b_forget[l].reshape(h_fox, 1)
        ga = norm_attn[l].reshape(1, d)
        gq = jnp.tile(q_norm[l], h_fox).reshape(1, w_sb)
        gk = jnp.tile(k_norm[l], h_fox).reshape(w_sb, 1)
        proj = lambda x2d, nb, tm: _proj_call(x2d, nb, tm, ga, wq, wkvf, bf, gq, gk, grp)
        gcat = jnp.concatenate([out_norm_sb[l], out_norm_fox[l]]).reshape(1, d)
        gm = norm_mlp[l].reshape(1, d)
        wo = w_out[l].astype(BF16)
        wu = w_up[l].astype(BF16)
        wd = w_down[l].astype(BF16)

        _, ktm, vtm, kt32_m, vt32_m, lft_m = proj(pad_rows(meta_tokens.astype(x_prompt.dtype)), 1, BLK)
        qb, ktb, vtb, kt32, vt32, lft = proj(xp, n_batch, 512)
        lft_meta = jnp.where(jnp.arange(BLK)[None, :] < N_META, lft_m[0], 0.0)
        c_main, c_meta = _cumsum_call(lft, lft_meta)
        qb3 = qb.reshape(n_batch, seq, d)
        o_sb, o_fx = _attn_call(qb3, ktb, vtb, ktm[0], vtm[0], c_main.reshape(n_batch, n_pairs, 2, seq),
                                c_meta.reshape(n_pairs, 2, LANES), n_pairs)
        xp = _mlp_call(o_sb.reshape(n_batch * seq, w_sb), o_fx.reshape(n_batch * seq, w_sb), xp, 512,
                       gcat, wo, gm, wu, wd)

        def with_meta(main, meta):
            m = jnp.broadcast_to(meta[:, :, :N_META], (n_batch,) + meta.shape[1:2] + (N_META,))
            return jnp.concatenate([m, main], axis=-1)

        tok_major = lambda a: jnp.transpose(a.reshape(n_batch, n_heads, HEAD_DIM, N_META + seq), (0, 3, 1, 2))
        outs[0].append(tok_major(with_meta(kt32, kt32_m)))
        outs[1].append(tok_major(with_meta(vt32, vt32_m)))
        outs[2].append(jnp.transpose(with_meta(lft, lft_m), (0, 2, 1)))

        qb_s, _, _, kt32_s, vt32_s, lft_s = proj(pad_rows(xs), 1, BLK)
        k_s = kt32_s[0, :, :n_dec].T.reshape(n_dec, n_heads, HEAD_DIM)
        v_s = vt32_s[0, :, :n_dec].T.reshape(n_dec, n_heads, HEAD_DIM)
        lf_s = lft_s[0, :, :n_dec].T
        qcol = qb_s[:n_dec].astype(F32).reshape(n_dec, n_heads, HEAD_DIM, 1)
        o_s = _decode_call(page_table, qcol, k_s[..., None], v_s[..., None], lf_s[..., None],
                           cache_kt, cache_vt, cache_lft, l, h_sb)
        o_s = o_s.reshape(n_dec, d)
        xs = _mlp_call(o_s[:, :w_sb], o_s[:, w_sb:], xs, n_dec, gcat, wo, gm, wu, wd)
        outs[3].append(k_s.reshape(n_dec, 1, n_heads, HEAD_DIM))
        outs[4].append(v_s.reshape(n_dec, 1, n_heads, HEAD_DIM))
        outs[5].append(lf_s.reshape(n_dec, 1, h_fox))

    y_prompt = xp.reshape(n_batch, seq, d)
    y_sample = xs.reshape(n_dec, 1, d)
    return (y_prompt, y_sample) + tuple(jnp.stack(o) for o in outs)
```
